```python
import math
import jax, jax.numpy as jnp
from jax import lax
import numpy as np

D_MODEL = 2048
BATCH = 1
SEQ = 8192
DEPTH = 2

Q_BLOCK = 128
ROPE_THETA = 500000.0
ROPE_FRACTION_DENOM = 4
NORM_EPS = 1e-6
A_HEADS = 4
A_QK_DIM = 64
A_V_DIM = 128
B_HEADS = 4
B_HEAD_DIM = 128
IDX_HEADS = 8
IDX_DIM = 64
IDX_TOPK_MAX = 256
C_HEADS = 8
C_NOPE_DIM = 128
C_ROPE_DIM = 64
C_QK_DIM = C_NOPE_DIM + C_ROPE_DIM
C_V_DIM = 128
C_KV_RANK = 512
N_BRANCHES = 3
A_OUT = A_HEADS * A_V_DIM
B_OUT = B_HEADS * B_HEAD_DIM
C_OUT = C_HEADS * C_V_DIM
MIX_WIDTH = A_OUT + B_OUT + C_OUT
IN_WIDTHS = (
    A_HEADS * 2 * A_QK_DIM,
    A_HEADS * 2 * A_QK_DIM,
    A_OUT,
    B_HEADS * B_HEAD_DIM,
    B_HEAD_DIM,
    B_HEAD_DIM,
    IDX_HEADS * IDX_DIM,
    IDX_DIM,
    IDX_HEADS,
    C_HEADS * C_QK_DIM,
    C_KV_RANK,
    C_ROPE_DIM,
    N_BRANCHES * D_MODEL,
)
IN_WIDTH = sum(IN_WIDTHS)
D_FF = ((8 * D_MODEL + 3 * 256 - 1) // (3 * 256)) * 256

kernel_name = 'hybrid_gated_diff_dsa_mla_block'


def _rmsnorm(x, g):
    xf = x.astype(jnp.float32)
    y = xf * lax.rsqrt(jnp.mean(xf * xf, axis=-1, keepdims=True) + NORM_EPS)
    return (y * g.astype(jnp.float32)).astype(x.dtype)


def _rope(x, rot_dim):
    seq = x.shape[1]
    half = rot_dim // 2
    pos = jnp.arange(seq, dtype=jnp.float32)
    inv_freq = ROPE_THETA ** (-jnp.arange(half, dtype=jnp.float32) * 2.0 / rot_dim)
    ang = pos[:, None] * inv_freq[None, :]
    cos = jnp.cos(ang)[None, :, None, :]
    sin = jnp.sin(ang)[None, :, None, :]
    xr = x[..., :rot_dim].astype(jnp.float32)
    x1, x2 = xr[..., :half], xr[..., half:]
    rot = jnp.concatenate([x1 * cos - x2 * sin, x2 * cos + x1 * sin], axis=-1).astype(x.dtype)
    return jnp.concatenate([rot, x[..., rot_dim:]], axis=-1)


def _split_cols(t, widths):
    cuts, acc = [], 0
    for w in widths[:-1]:
        acc += w
        cuts.append(acc)
    return jnp.split(t, cuts, axis=-1)


def _masked_softmax(s, scale, mask):
    s = jnp.where(mask, s.astype(jnp.float32) * scale, -jnp.inf)
    return jax.nn.softmax(s, axis=-1)


def _causal_mask(start, seq):
    qpos = start + jnp.arange(Q_BLOCK, dtype=jnp.int32)
    mask = jnp.arange(seq, dtype=jnp.int32)[None, :] <= qpos[:, None]
    return qpos, mask


def _sweep_query_blocks(block_fn, seq):
    starts = jnp.arange(seq // Q_BLOCK, dtype=jnp.int32) * Q_BLOCK
    out = lax.map(block_fn, starts)
    out = jnp.moveaxis(out, 0, 1)
    return out.reshape(out.shape[:1] + (seq,) + out.shape[3:])


def _diff_attention(q1, q2, k1, k2, v, lam):
    seq = q1.shape[1]
    scale = A_QK_DIM ** -0.5

    def block(start):
        _, mask = _causal_mask(start, seq)
        q1b = lax.dynamic_slice_in_dim(q1, start, Q_BLOCK, axis=1)
        q2b = lax.dynamic_slice_in_dim(q2, start, Q_BLOCK, axis=1)
        p1 = _masked_softmax(jnp.einsum('bqhd,bkhd->bhqk', q1b, k1), scale, mask)
        p2 = _masked_softmax(jnp.einsum('bqhd,bkhd->bhqk', q2b, k2), scale, mask)
        p = (p1 - lam * p2).astype(v.dtype)
        return jnp.einsum('bhqk,bkhd->bqhd', p, v)

    return _sweep_query_blocks(block, seq)


def _dense_attention(q, k, v, scale):
    seq = q.shape[1]

    def block(start):
        _, mask = _causal_mask(start, seq)
        qb = lax.dynamic_slice_in_dim(q, start, Q_BLOCK, axis=1)
        p = _masked_softmax(jnp.einsum('bqhd,bkhd->bhqk', qb, k), scale, mask)
        return jnp.einsum('bhqk,bkhd->bqhd', p.astype(v.dtype), v)

    return _sweep_query_blocks(block, seq)


def _dsa_attention(q, k, v, q_idx, k_idx, w_idx):
    seq = k.shape[1]
    top_k = min(IDX_TOPK_MAX, seq // 4)
    scale = B_HEAD_DIM ** -0.5
    gather = jax.vmap(lambda t, i: t[i])

    def block(start):
        qpos, mask = _causal_mask(start, seq)
        qib = lax.dynamic_slice_in_dim(q_idx, start, Q_BLOCK, axis=1)
        wb = lax.dynamic_slice_in_dim(w_idx, start, Q_BLOCK, axis=1)
        rel = jax.nn.relu(jnp.einsum('bqhd,bkd->bqhk', qib, k_idx).astype(jnp.float32))
        score = jnp.einsum('bqh,bqhk->bqk', wb.astype(jnp.float32), rel)
        score = jnp.where(mask, score, -jnp.inf)
        _, sel = lax.top_k(score, top_k)
        valid = sel <= qpos[None, :, None]
        kg = gather(k, sel)
        vg = gather(v, sel)
        qb = lax.dynamic_slice_in_dim(q, start, Q_BLOCK, axis=1)
        p = _masked_softmax(jnp.einsum('bqhd,bqkd->bqhk', qb, kg), scale, valid[:, :, None, :])
        return jnp.einsum('bqhk,bqkd->bqhd', p.astype(vg.dtype), vg)

    return _sweep_query_blocks(block, seq)


def _hybrid_mixer(h, lam_init, w_in, a_q_norm, a_k_norm, a_lambda, a_sub_norm, b_q_norm, b_k_norm,
                  idx_k_norm, c_q_norm, c_kv_norm, w_kv_b, c_k_norm, w_branch, w_out):
    bsz, seq, _ = h.shape
    (a_q, a_k, a_v, b_q, b_k, b_v, i_q, i_k, i_w, c_q, c_kv, c_kr, gates) = _split_cols(h @ w_in, IN_WIDTHS)

    a_rot = A_QK_DIM // ROPE_FRACTION_DENOM
    aq = _rope(_rmsnorm(a_q.reshape(bsz, seq, 2 * A_HEADS, A_QK_DIM), a_q_norm), a_rot)
    ak = _rope(_rmsnorm(a_k.reshape(bsz, seq, 2 * A_HEADS, A_QK_DIM), a_k_norm), a_rot)
    aq = aq.reshape(bsz, seq, A_HEADS, 2, A_QK_DIM)
    ak = ak.reshape(bsz, seq, A_HEADS, 2, A_QK_DIM)
    av = a_v.reshape(bsz, seq, A_HEADS, A_V_DIM)
    lf = a_lambda.astype(jnp.float32)
    lam = jnp.exp(jnp.sum(lf[0] * lf[1])) - jnp.exp(jnp.sum(lf[2] * lf[3])) + lam_init
    a_o = _diff_attention(aq[..., 0, :], aq[..., 1, :], ak[..., 0, :], ak[..., 1, :], av, lam)
    a_o = (_rmsnorm(a_o, a_sub_norm) * (1.0 - lam_init)).reshape(bsz, seq, A_OUT)

    b_rot = B_HEAD_DIM // ROPE_FRACTION_DENOM
    i_rot = IDX_DIM // ROPE_FRACTION_DENOM
    bq = _rope(_rmsnorm(b_q.reshape(bsz, seq, B_HEADS, B_HEAD_DIM), b_q_norm), b_rot)
    bk = _rope(_rmsnorm(b_k.reshape(bsz, seq, 1, B_HEAD_DIM), b_k_norm), b_rot)[:, :, 0]
    iq = _rope(i_q.reshape(bsz, seq, IDX_HEADS, IDX_DIM), i_rot)
    ik = _rope(_rmsnorm(i_k, idx_k_norm)[:, :, None, :], i_rot)[:, :, 0]
    iw = i_w * (IDX_HEADS ** -0.5 * IDX_DIM ** -0.5)
    b_o = _dsa_attention(bq, bk, b_v, iq, ik, iw).reshape(bsz, seq, B_OUT)

    cq = _rope(_rmsnorm(c_q.reshape(bsz, seq, C_HEADS, C_QK_DIM), c_q_norm), C_ROPE_DIM)
    kv = (_rmsnorm(c_kv, c_kv_norm) @ w_kv_b).reshape(bsz, seq, C_HEADS, C_NOPE_DIM + C_V_DIM)
    k_nope, c_v = kv[..., :C_NOPE_DIM], kv[..., C_NOPE_DIM:]
    k_rope = jnp.broadcast_to(c_kr[:, :, None, :], (bsz, seq, C_HEADS, C_ROPE_DIM))
    ck = _rope(_rmsnorm(jnp.concatenate([k_rope, k_nope], axis=-1), c_k_norm), C_ROPE_DIM)
    c_o = _dense_attention(cq, ck, c_v, C_QK_DIM ** -0.5).reshape(bsz, seq, C_OUT)

    g = jax.nn.sigmoid(gates.reshape(bsz, seq, N_BRANCHES, D_MODEL).astype(jnp.float32)).astype(h.dtype)
    w_a = w_branch[:A_OUT]
    w_b = w_branch[A_OUT:A_OUT + B_OUT]
    w_c = w_branch[A_OUT + B_OUT:]
    merged = g[:, :, 0] * (a_o @ w_a) + g[:, :, 1] * (b_o @ w_b) + g[:, :, 2] * (c_o @ w_c)
    return merged @ w_out


def setup_inputs(seed: int = 0) -> dict:
    key = jax.random.key(seed)
    ks = jax.random.split(key, 20)

    def nrm(k, shape, scale):
        return jax.random.normal(k, shape, jnp.float32) * scale

    def gain(k, n):
        return 1.0 + 0.02 * jax.random.normal(k, (DEPTH, n), jnp.float32)

    return {
        'x': nrm(ks[0], (BATCH, SEQ, D_MODEL), 1.0),
        'attn_norm': gain(ks[1], D_MODEL),
        'w_in': nrm(ks[2], (DEPTH, D_MODEL, IN_WIDTH), D_MODEL ** -0.5),
        'a_q_norm': gain(ks[3], A_QK_DIM),
        'a_k_norm': gain(ks[4], A_QK_DIM),
        'a_lambda': nrm(ks[5], (DEPTH, 4, A_QK_DIM), 0.1),
        'a_sub_norm': gain(ks[6], A_V_DIM),
        'b_q_norm': gain(ks[7], B_HEAD_DIM),
        'b_k_norm': gain(ks[8], B_HEAD_DIM),
        'idx_k_norm': gain(ks[9], IDX_DIM),
        'c_q_norm': gain(ks[10], C_QK_DIM),
        'c_kv_norm': gain(ks[11], C_KV_RANK),
        'w_kv_b': nrm(ks[12], (DEPTH, C_KV_RANK, C_HEADS * (C_NOPE_DIM + C_V_DIM)), C_KV_RANK ** -0.5),
        'c_k_norm': gain(ks[13], C_QK_DIM),
        'w_branch': nrm(ks[14], (DEPTH, MIX_WIDTH, D_MODEL), MIX_WIDTH ** -0.5),
        'w_out': nrm(ks[15], (DEPTH, D_MODEL, D_MODEL), D_MODEL ** -0.5),
        'ffn_norm': gain(ks[16], D_MODEL),
        'w_gate_up': nrm(ks[17], (DEPTH, D_MODEL, 2 * D_FF), D_MODEL ** -0.5),
        'w_down': nrm(ks[18], (DEPTH, D_FF, D_MODEL), D_FF ** -0.5),
    }


def reference(x, attn_norm, w_in, a_q_norm, a_k_norm, a_lambda, a_sub_norm, b_q_norm, b_k_norm,
              idx_k_norm, c_q_norm, c_kv_norm, w_kv_b, c_k_norm, w_branch, w_out, ffn_norm,
              w_gate_up, w_down):
    for l in range(DEPTH):
        lam_init = 0.8 - 0.6 * math.exp(-0.3 * l)
        h = _rmsnorm(x, attn_norm[l])
        x = x + _hybrid_mixer(h, lam_init, w_in[l], a_q_norm[l], a_k_norm[l], a_lambda[l], a_sub_norm[l],
                              b_q_norm[l], b_k_norm[l], idx_k_norm[l], c_q_norm[l], c_kv_norm[l],
                              w_kv_b[l], c_k_norm[l], w_branch[l], w_out[l])
        h = _rmsnorm(x, ffn_norm[l])
        gate, up = jnp.split(h @ w_gate_up[l], 2, axis=-1)
        x = x + (jax.nn.silu(gate) * up) @ w_down[l]
    return x
```

```python
import math
from functools import partial

import jax
import jax.numpy as jnp
from jax import lax
from jax.experimental import pallas as pl
from jax.experimental.pallas import tpu as pltpu

F32 = jnp.float32
BF16 = jnp.bfloat16
I32 = jnp.int32

ROPE_THETA = 500000.0
NORM_EPS = 1e-6
LOG2E = math.log2(math.e)
LANES = 128
VMEM_LIMIT = 56 * 1024 * 1024

A_HEADS, A_QK, A_V = 4, 64, 128
B_HEADS, B_DIM = 4, 128
I_HEADS, I_DIM, TOPK_MAX = 8, 64, 256
C_HEADS, C_NOPE, C_ROPE, C_V, C_RANK = 8, 128, 64, 128, 512
C_QK = C_NOPE + C_ROPE
N_BRANCH = 3
INT_MIN = -2 ** 31

COL_A_Q, COL_A_K, COL_A_V, COL_B_Q, COL_I_Q = 0, 512, 1024, 1536, 2048
COL_C_KV, COL_C_QN, COL_C_QR = 2560, 3072, 4096
COL_B_K, COL_B_V, COL_IKR, COL_IW, COL_G = 4608, 4736, 4864, 4992, 5120


def _params(sem):
    return pltpu.CompilerParams(dimension_semantics=sem, vmem_limit_bytes=VMEM_LIMIT)


def _dot(a, b):
    return jnp.dot(a, b, preferred_element_type=F32)


def _dot_nt(a, b):
    return lax.dot_general(a, b, (((1,), (1,)), ((), ())), preferred_element_type=F32)


def _rms_rows(x, g):
    ms = jnp.mean(x * x, axis=-1, keepdims=True)
    return x * lax.rsqrt(ms + NORM_EPS) * g


def _norm_matmul_kernel(x_ref, g_ref, w_ref, o_ref, xn_ref):
    @pl.when(pl.program_id(1) == 0)
    def _():
        xn_ref[...] = _rms_rows(x_ref[...].astype(F32), g_ref[...]).astype(BF16)

    o_ref[...] = _dot(xn_ref[...], w_ref[...]).astype(o_ref.dtype)


def _norm_matmul(x, g, w, *, x_col_block, k, tm, tn, out_dtype, name):
    s = x.shape[0]
    n = w.shape[1]
    return pl.pallas_call(
        _norm_matmul_kernel,
        grid=(s // tm, n // tn),
        in_specs=[
            pl.BlockSpec((tm, k), lambda i, j: (i, x_col_block)),
            pl.BlockSpec((1, k), lambda i, j: (0, 0)),
            pl.BlockSpec((k, tn), lambda i, j: (0, j)),
        ],
        out_specs=pl.BlockSpec((tm, tn), lambda i, j: (i, j)),
        out_shape=jax.ShapeDtypeStruct((s, n), out_dtype),
        scratch_shapes=[pltpu.VMEM((tm, k), BF16)],
        compiler_params=_params(("arbitrary", "arbitrary")),
        name=name,
    )(x, g, w)


def _ffn_up_kernel(x_ref, g_ref, wg_ref, wu_ref, o_ref, xn_ref):
    @pl.when(pl.program_id(1) == 0)
    def _():
        xn_ref[...] = _rms_rows(x_ref[...], g_ref[...]).astype(BF16)

    xn = xn_ref[...]
    gate = _dot(xn, wg_ref[...])
    up = _dot(xn, wu_ref[...])
    o_ref[...] = (gate * (1.0 / (1.0 + jnp.exp(-gate))) * up).astype(o_ref.dtype)


def _ffn_up(x, g, w_gate_up, *, tm, tn):
    s, d = x.shape
    d_ff = w_gate_up.shape[1] // 2
    nj = d_ff // tn
    return pl.pallas_call(
        _ffn_up_kernel,
        grid=(s // tm, nj),
        in_specs=[
            pl.BlockSpec((tm, d), lambda i, j: (i, 0)),
            pl.BlockSpec((1, d), lambda i, j: (0, 0)),
            pl.BlockSpec((d, tn), lambda i, j: (0, j)),
            pl.BlockSpec((d, tn), lambda i, j: (0, j + nj)),
        ],
        out_specs=pl.BlockSpec((tm, tn), lambda i, j: (i, j)),
        out_shape=jax.ShapeDtypeStruct((s, d_ff), BF16),
        scratch_shapes=[pltpu.VMEM((tm, d), BF16)],
        compiler_params=_params(("arbitrary", "arbitrary")),
        name="ffn_up",
    )(x, g, w_gate_up, w_gate_up)


def _matmul_residual_kernel(a_ref, w_ref, r_ref, o_ref):
    o_ref[...] = r_ref[...] + _dot(a_ref[...], w_ref[...])


def _matmul_residual(a, w, res, *, tm, tn, name):
    s, k = a.shape
    n = w.shape[1]
    return pl.pallas_call(
        _matmul_residual_kernel,
        grid=(s // tm, n // tn),
        in_specs=[
            pl.BlockSpec((tm, k), lambda i, j: (i, 0)),
            pl.BlockSpec((k, tn), lambda i, j: (0, j)),
            pl.BlockSpec((tm, tn), lambda i, j: (i, j)),
        ],
        out_specs=pl.BlockSpec((tm, tn), lambda i, j: (i, j)),
        out_shape=jax.ShapeDtypeStruct((s, n), F32),
        compiler_params=_params(("arbitrary", "arbitrary")),
        name=name,
    )(a, w, res)


def _sigmoid(x):
    return 1.0 / (1.0 + jnp.exp(-x))


def _merge_kernel(a_ref, b_ref, c_ref, wa_ref, wb_ref, wc_ref, ga_ref, gb_ref, gc_ref, o_ref):
    merged = _sigmoid(ga_ref[...]) * _dot(a_ref[...], wa_ref[...])
    merged += _sigmoid(gb_ref[...]) * _dot(b_ref[...], wb_ref[...])
    merged += _sigmoid(gc_ref[...]) * _dot(c_ref[...], wc_ref[...])
    o_ref[...] = merged.astype(o_ref.dtype)


def _merge(a_o, b_o, c_o, w_branch, proj, *, d, tm, tn):
    s = a_o.shape[0]
    a_w, b_w, c_w = a_o.shape[1], b_o.shape[1], c_o.shape[1]
    assert a_w == b_w and c_w == a_w + b_w
    g0 = COL_G // tn
    gstep = d // tn
    return pl.pallas_call(
        _merge_kernel,
        grid=(s // tm, d // tn),
        in_specs=[
            pl.BlockSpec((tm, a_w), lambda i, j: (i, 0)),
            pl.BlockSpec((tm, b_w), lambda i, j: (i, 0)),
            pl.BlockSpec((tm, c_w), lambda i, j: (i, 0)),
            pl.BlockSpec((a_w, tn), lambda i, j: (0, j)),
            pl.BlockSpec((b_w, tn), lambda i, j: (1, j)),
            pl.BlockSpec((c_w, tn), lambda i, j: (1, j)),
            pl.BlockSpec((tm, tn), lambda i, j: (i, g0 + j)),
            pl.BlockSpec((tm, tn), lambda i, j: (i, g0 + gstep + j)),
            pl.BlockSpec((tm, tn), lambda i, j: (i, g0 + 2 * gstep + j)),
        ],
        out_specs=pl.BlockSpec((tm, tn), lambda i, j: (i, j)),
        out_shape=jax.ShapeDtypeStruct((s, d), BF16),
        compiler_params=_params(("arbitrary", "arbitrary")),
        name="merge",
    )(a_o, b_o, c_o, w_branch, w_branch, w_branch, proj, proj, proj)


def _lane(shape):
    return lax.broadcasted_iota(I32, shape, 1)


def _half_sums(sq):
    low = _lane(sq.shape) < 64
    lo = jnp.sum(jnp.where(low, sq, 0.0), axis=-1, keepdims=True)
    hi = jnp.sum(jnp.where(low, 0.0, sq), axis=-1, keepdims=True)
    return lo, hi


def _norm64(x, g):
    lo, hi = _half_sums(x * x)
    ms = jnp.where(_lane(x.shape) < 64, lo, hi) * (1.0 / 64.0)
    return x * lax.rsqrt(ms + NORM_EPS) * g


def _norm128(x, g):
    ms = jnp.mean(x * x, axis=-1, keepdims=True)
    return x * lax.rsqrt(ms + NORM_EPS) * g


def _rope(y, cos, sin_up, sin_dn, half):
    return (y * cos + pltpu.roll(y, LANES - half, 1) * sin_up
            + pltpu.roll(y, half, 1) * sin_dn)


def _chunk(ref, c):
    return ref[:, c * LANES:(c + 1) * LANES]


def _prep_a_kernel(q_ref, k_ref, v_ref, gq_ref, gk_ref, cos_ref, su_ref, sd_ref,
                   oq_ref, ok_ref, ov_ref, *, q_scale):
    cos, su, sd = cos_ref[...], su_ref[...], sd_ref[...]
    half = A_QK // 8
    for c in range(A_HEADS):
        sl = slice(c * LANES, (c + 1) * LANES)
        q = _rope(_norm64(_chunk(q_ref, c), gq_ref[...]), cos, su, sd, half)
        k = _rope(_norm64(_chunk(k_ref, c), gk_ref[...]), cos, su, sd, half)
        oq_ref[:, sl] = (q * q_scale).astype(BF16)
        ok_ref[:, sl] = k.astype(BF16)
    ov_ref[...] = v_ref[...].astype(BF16)


def _prep_b_kernel(q_ref, k_ref, v_ref, iq_ref, ikr_ref, iw_ref,
                   gq_ref, gk_ref, gik_ref,
                   cb_ref, sub_ref, sdb_ref, ca_ref, sua_ref, sda_ref,
                   oq_ref, ok_ref, ov_ref, oiq_ref, oik_ref, oiw_ref, *, q_scale, w_scale):
    cb, sub, sdb = cb_ref[...], sub_ref[...], sdb_ref[...]
    ca, sua, sda = ca_ref[...], sua_ref[...], sda_ref[...]
    b_half = B_DIM // 8
    i_half = I_DIM // 8
    for c in range(B_HEADS):
        sl = slice(c * LANES, (c + 1) * LANES)
        q = _rope(_norm128(_chunk(q_ref, c), gq_ref[...]), cb, sub, sdb, b_half)
        oq_ref[:, sl] = (q * q_scale).astype(BF16)
        oiq_ref[:, sl] = _rope(_chunk(iq_ref, c), ca, sua, sda, i_half).astype(BF16)
    ok_ref[...] = _rope(_norm128(k_ref[...], gk_ref[...]), cb, sub, sdb, b_half).astype(BF16)
    ov_ref[...] = v_ref[...].astype(BF16)
    x = ikr_ref[...]
    low = _lane(x.shape) < 64
    ik = _rope(_norm64(x, gik_ref[...]), ca, sua, sda, i_half)
    ik = jnp.where(low, ik, 0.0)
    oik_ref[...] = (ik + pltpu.roll(ik, 64, 1)).astype(BF16)
    oiw_ref[...] = iw_ref[...] * w_scale


def _prep_c_kernel(qn_ref, qr_ref, ikr_ref, kv_ref,
                   gqn_ref, gqr_ref, gkn_ref, gkr_ref, cos_ref, su_ref, sd_ref,
                   oq_ref, ok_ref, ov_ref, *, q_scale):
    cos, su, sd = cos_ref[...], su_ref[...], sd_ref[...]
    half = C_ROPE // 2
    low = _lane(cos.shape) < 64
    inv_d = 1.0 / C_QK

    for c in range(C_HEADS // 2):
        r = _chunk(qr_ref, c)
        lo, hi = _half_sums(r * r)
        rr = _rope(r * gqr_ref[...], cos, su, sd, half)
        parts = (jnp.where(low, rr, 0.0), pltpu.roll(jnp.where(low, 0.0, rr), 64, 1))
        for j, (ss_r, rope_part) in enumerate(zip((lo, hi), parts)):
            h = 2 * c + j
            n = _chunk(qn_ref, h)
            ss = jnp.sum(n * n, axis=-1, keepdims=True) + ss_r
            sc = lax.rsqrt(ss * inv_d + NORM_EPS) * q_scale
            oq_ref[:, 2 * h * LANES:(2 * h + 1) * LANES] = (n * sc * gqn_ref[...]).astype(BF16)
            oq_ref[:, (2 * h + 1) * LANES:(2 * h + 2) * LANES] = (rope_part * sc).astype(BF16)

    x = ikr_ref[...]
    kr = pltpu.roll(jnp.where(_lane(x.shape) < 64, 0.0, x), 64, 1)
    ss_r = jnp.sum(kr * kr, axis=-1, keepdims=True)
    kr = _rope(kr * gkr_ref[...], cos, su, sd, half)
    for h in range(C_HEADS):
        n = _chunk(kv_ref, h)
        ss = jnp.sum(n * n, axis=-1, keepdims=True) + ss_r
        sc = lax.rsqrt(ss * inv_d + NORM_EPS)
        ok_ref[:, 2 * h * LANES:(2 * h + 1) * LANES] = (n * sc * gkn_ref[...]).astype(BF16)
        ok_ref[:, (2 * h + 1) * LANES:(2 * h + 2) * LANES] = (kr * sc).astype(BF16)
    v0 = C_HEADS * C_NOPE
    ov_ref[...] = kv_ref[:, v0:v0 + C_HEADS * C_V].astype(BF16)


def _row_spec(ts, width, col_block):
    return pl.BlockSpec((ts, width), lambda i: (i, col_block))


def _const_spec(shape):
    return pl.BlockSpec(shape, lambda i: (0,) * len(shape))


def _prep_a(proj, gq, gk, tab, *, ts):
    s = proj.shape[0]
    w = A_HEADS * 2 * A_QK
    tspec = _row_spec(ts, LANES, 0)
    out = jax.ShapeDtypeStruct((s, w), BF16)
    return pl.pallas_call(
        partial(_prep_a_kernel, q_scale=A_QK ** -0.5 * LOG2E),
        grid=(s // ts,),
        in_specs=[_row_spec(ts, w, COL_A_Q // w), _row_spec(ts, w, COL_A_K // w),
                  _row_spec(ts, w, COL_A_V // w),
                  _const_spec((1, LANES)), _const_spec((1, LANES)), tspec, tspec, tspec],
        out_specs=[_row_spec(ts, w, 0)] * 3,
        out_shape=[out, out, out],
        compiler_params=_params(("arbitrary",)),
        name="prep_a",
    )(proj, proj, proj, gq, gk, *tab)


def _prep_b(proj, gq, gk, gik, tab_b, tab_a, *, ts):
    s = proj.shape[0]
    w = B_HEADS * B_DIM
    tspec = _row_spec(ts, LANES, 0)
    gspec = _const_spec((1, LANES))
    wide = jax.ShapeDtypeStruct((s, w), BF16)
    narrow = jax.ShapeDtypeStruct((s, LANES), BF16)
    return pl.pallas_call(
        partial(_prep_b_kernel, q_scale=B_DIM ** -0.5 * LOG2E,
                w_scale=I_HEADS ** -0.5 * I_DIM ** -0.5),
        grid=(s // ts,),
        in_specs=[_row_spec(ts, w, COL_B_Q // w), _row_spec(ts, LANES, COL_B_K // LANES),
                  _row_spec(ts, LANES, COL_B_V // LANES), _row_spec(ts, w, COL_I_Q // w),
                  _row_spec(ts, LANES, COL_IKR // LANES), _row_spec(ts, LANES, COL_IW // LANES),
                  gspec, gspec, gspec] + [tspec] * 6,
        out_specs=[_row_spec(ts, w, 0), tspec, tspec, _row_spec(ts, w, 0), tspec, tspec],
        out_shape=[wide, narrow, narrow, wide, narrow,
                   jax.ShapeDtypeStruct((s, LANES), F32)],
        compiler_params=_params(("arbitrary",)),
        name="prep_b",
    )(proj, proj, proj, proj, proj, proj, gq, gk, gik, *tab_b, *tab_a)


def _prep_c(proj, kv, gqn, gqr, gkn, gkr, tab, *, ts):
    s = proj.shape[0]
    wn, wr = C_HEADS * C_NOPE, C_HEADS * C_ROPE
    wqk = C_HEADS * 2 * LANES
    tspec = _row_spec(ts, LANES, 0)
    gspec = _const_spec((1, LANES))
    return pl.pallas_call(
        partial(_prep_c_kernel, q_scale=C_QK ** -0.5 * LOG2E),
        grid=(s // ts,),
        in_specs=[_row_spec(ts, wn, COL_C_QN // wn), _row_spec(ts, wr, COL_C_QR // wr),
                  _row_spec(ts, LANES, COL_IKR // LANES), _row_spec(ts, kv.shape[1], 0),
                  gspec, gspec, gspec, gspec, tspec, tspec, tspec],
        out_specs=[_row_spec(ts, wqk, 0), _row_spec(ts, wqk, 0), _row_spec(ts, C_HEADS * C_V, 0)],
        out_shape=[jax.ShapeDtypeStruct((s, wqk), BF16), jax.ShapeDtypeStruct((s, wqk), BF16),
                   jax.ShapeDtypeStruct((s, C_HEADS * C_V), BF16)],
        compiler_params=_params(("arbitrary",)),
        name="prep_c",
    )(proj, proj, proj, kv, gqn, gqr, gkn, gkr, *tab)


def _flash_step(q, kc, vc, carry, mask):
    m, l, acc = carry
    s = _dot_nt(q, kc)
    if mask is not None:
        s = jnp.where(mask, s, -jnp.inf)
    m_new = jnp.maximum(m, jnp.max(s, axis=-1, keepdims=True))
    alpha = jnp.exp2(m - m_new)
    p = jnp.exp2(s - m_new)
    l = alpha * l + jnp.sum(p, axis=-1, keepdims=True)
    acc = alpha * acc + _dot(p.astype(BF16), vc)
    return m_new, l, acc


def _flash_init(rows, dv, m0):
    return (jnp.full((rows, 1), m0, F32), jnp.zeros((rows, 1), F32), jnp.zeros((rows, dv), F32))


def _causal_flash(q, k_ref, v_ref, i, t, reps):
    rows = reps * t

    def body(c, carry):
        return _flash_step(q, k_ref[c], v_ref[c], carry, None)

    carry = lax.fori_loop(0, i, body, _flash_init(rows, v_ref.shape[-1], -jnp.inf))
    assert reps in (1, 2)
    row = lax.broadcasted_iota(I32, (rows, t), 0)
    if reps == 2:
        row = jnp.where(row >= t, row - t, row)
    col = lax.broadcasted_iota(I32, (rows, t), 1)
    return _flash_step(q, k_ref[i], v_ref[i], carry, col <= row)


def _attn_c_kernel(q_ref, k_ref, v_ref, o_ref, *, t):
    _, l, acc = _causal_flash(q_ref[...], k_ref, v_ref, pl.program_id(1), t, 1)
    o_ref[...] = (acc / l).astype(o_ref.dtype)


def _attn_c(cq, ck, cv, *, t):
    s = cq.shape[0]
    nc = s // t
    dq = 2 * LANES
    ck3 = ck.reshape(nc, t, ck.shape[1])
    cv3 = cv.reshape(nc, t, cv.shape[1])
    return pl.pallas_call(
        partial(_attn_c_kernel, t=t),
        grid=(C_HEADS, nc),
        in_specs=[pl.BlockSpec((t, dq), lambda h, i: (i, h)),
                  pl.BlockSpec((nc, t, dq), lambda h, i: (0, 0, h)),
                  pl.BlockSpec((nc, t, C_V), lambda h, i: (0, 0, h))],
        out_specs=pl.BlockSpec((t, C_V), lambda h, i: (i, h)),
        out_shape=jax.ShapeDtypeStruct((s, C_HEADS * C_V), BF16),
        compiler_params=_params(("arbitrary", "arbitrary")),
        name="attn_c",
    )(cq, ck3, cv3)


def _attn_a_kernel(lam_ref, g_ref, q_ref, k_ref, v_ref, o_ref, *, t, lam_init):
    q = q_ref[...]
    low = _lane(q.shape) < A_QK
    zero = jnp.zeros_like(q)
    qq = jnp.concatenate([jnp.where(low, q, zero), jnp.where(low, zero, q)], axis=0)
    _, l, acc = _causal_flash(qq, k_ref, v_ref, pl.program_id(1), t, 2)
    lf = lam_ref[...]
    lam = (jnp.exp(jnp.sum(lf[0:1, :] * lf[1:2, :], axis=-1, keepdims=True))
           - jnp.exp(jnp.sum(lf[2:3, :] * lf[3:4, :], axis=-1, keepdims=True)) + lam_init)
    o = acc[:t] / l[:t] - lam * (acc[t:] / l[t:])
    o_ref[...] = (_norm128(o, g_ref[...]) * (1.0 - lam_init)).astype(o_ref.dtype)


def _attn_a(aq, ak, av, a_lambda, g_sub, *, t, lam_init):
    s = aq.shape[0]
    nc = s // t
    ak3 = ak.reshape(nc, t, ak.shape[1])
    av3 = av.reshape(nc, t, av.shape[1])
    return pl.pallas_call(
        partial(_attn_a_kernel, t=t, lam_init=lam_init),
        grid=(A_HEADS, nc),
        in_specs=[pl.BlockSpec(a_lambda.shape, lambda h, i: (0, 0)),
                  pl.BlockSpec((1, LANES), lambda h, i: (0, 0)),
                  pl.BlockSpec((t, LANES), lambda h, i: (i, h)),
                  pl.BlockSpec((nc, t, LANES), lambda h, i: (0, 0, h)),
                  pl.BlockSpec((nc, t, A_V), lambda h, i: (0, 0, h))],
        out_specs=pl.BlockSpec((t, A_V), lambda h, i: (i, h)),
        out_shape=jax.ShapeDtypeStruct((s, A_HEADS * A_V), BF16),
        compiler_params=_params(("arbitrary", "arbitrary")),
        name="attn_a",
    )(a_lambda, g_sub, aq, ak3, av3)


def _sortable_key(score):
    bits = lax.bitcast_convert_type(jnp.where(score == 0.0, 0.0, score), I32)
    return bits ^ (lax.shift_right_arithmetic(bits, 31) & 0x7FFFFFFF)


def _attn_b_kernel(iq_ref, iw_ref, ik_ref, q_ref, k_ref, v_ref, o_ref, key_ref,
                   *, tq, tk, top_k):
    i = pl.program_id(0)
    nchunks = ((i + 1) * tq + tk - 1) // tk
    row_g = i * tq + lax.broadcasted_iota(I32, (tq, tk), 0)
    col_l = lax.broadcasted_iota(I32, (tq, tk), 1)

    iq = iq_ref[...]
    low = (_lane(iq.shape) & (LANES - 1)) < I_DIM
    zero = jnp.zeros_like(iq)
    iq_lo = jnp.where(low, iq, zero)
    iq_hi = jnp.where(low, zero, iq)
    iw = iw_ref[...]

    def score_body(c, _):
        kc = ik_ref[c]
        score = jnp.zeros((tq, tk), F32)
        for j in range(I_HEADS // 2):
            sl = slice(j * LANES, (j + 1) * LANES)
            score += iw[:, 2 * j:2 * j + 1] * jnp.maximum(_dot_nt(iq_lo[:, sl], kc), 0.0)
            score += iw[:, 2 * j + 1:2 * j + 2] * jnp.maximum(_dot_nt(iq_hi[:, sl], kc), 0.0)
        key = jnp.where(c * tk + col_l <= row_g, _sortable_key(score), INT_MIN)
        key_ref[c] = key
        return 0

    lax.fori_loop(0, nchunks, score_body, 0)

    def count(pred):
        def body(c, acc):
            hit = jnp.where(pred(key_ref[c]), 1, 0)
            for g in range(tk // LANES):
                acc += hit[:, g * LANES:(g + 1) * LANES]
            return acc
        acc = lax.fori_loop(0, nchunks, body, jnp.zeros((tq, LANES), I32))
        return jnp.sum(acc, axis=-1, keepdims=True)

    thr = jnp.where(count(lambda k: k >= 0) >= top_k, 0, INT_MIN).astype(I32)

    def bisect(b, thr):
        cand = thr + lax.shift_left(jnp.int32(1), 30 - b)
        return jnp.where(count(lambda k: k >= cand) >= top_k, cand, thr)

    thr = lax.fori_loop(0, 31, bisect, thr)
    thr = jnp.maximum(thr, INT_MIN + 1)
    need = (top_k - count(lambda k: k > thr)).astype(F32)

    tri = (lax.broadcasted_iota(I32, (tk, tk), 0) <= lax.broadcasted_iota(I32, (tk, tk), 1)).astype(BF16)
    q = q_ref[...]
    dv = v_ref.shape[-1]
    neg = -1e30

    def attn_body(c, carry):
        seen, states = carry
        key = key_ref[c]
        tie = key == thr
        rank = seen + _dot(jnp.where(tie, 1.0, 0.0).astype(BF16), tri)
        sel = (key > thr) | (tie & (rank <= need))
        seen = seen + jnp.sum(jnp.where(tie, 1.0, 0.0), axis=-1, keepdims=True)
        kc, vc = k_ref[c], v_ref[c]
        new_states = []
        for h in range(B_HEADS):
            m, l, acc = states[h]
            s = jnp.where(sel, _dot_nt(q[:, h * LANES:(h + 1) * LANES], kc), neg)
            m_new = jnp.maximum(m, jnp.max(s, axis=-1, keepdims=True))
            alpha = jnp.exp2(m - m_new)
            p = jnp.exp2(s - m_new)
            l = alpha * l + jnp.sum(p, axis=-1, keepdims=True)
            acc = alpha * acc + _dot(p.astype(BF16), vc)
            new_states.append((m_new, l, acc))
        return seen, tuple(new_states)

    init = (jnp.zeros((tq, 1), F32), tuple(_flash_init(tq, dv, neg) for _ in range(B_HEADS)))
    _, states = lax.fori_loop(0, nchunks, attn_body, init)
    for h in range(B_HEADS):
        _, l, acc = states[h]
        o_ref[:, h * dv:(h + 1) * dv] = (acc / l).astype(o_ref.dtype)


def _attn_b(iq, iw, ik2, bq, bk, bv, *, tq, tk, top_k):
    s = bq.shape[0]
    nc = s // tk
    ik3 = ik2.reshape(nc, tk, LANES)
    bk3 = bk.reshape(nc, tk, B_DIM)
    bv3 = bv.reshape(nc, tk, B_DIM)
    w = B_HEADS * B_DIM
    whole = lambda shape: pl.BlockSpec(shape, lambda i: (0, 0, 0))
    return pl.pallas_call(
        partial(_attn_b_kernel, tq=tq, tk=tk, top_k=top_k),
        grid=(s // tq,),
        in_specs=[pl.BlockSpec((tq, I_HEADS * I_DIM), lambda i: (i, 0)),
                  pl.BlockSpec((tq, LANES), lambda i: (i, 0)),
                  whole((nc, tk, LANES)),
                  pl.BlockSpec((tq, w), lambda i: (i, 0)),
                  whole((nc, tk, B_DIM)), whole((nc, tk, B_DIM))],
        out_specs=pl.BlockSpec((tq, w), lambda i: (i, 0)),
        out_shape=jax.ShapeDtypeStruct((s, w), BF16),
        scratch_shapes=[pltpu.VMEM((nc, tq, tk), I32)],
        compiler_params=_params(("arbitrary",)),
        name="attn_b",
    )(iq, iw, ik3, bq, bk3, bv3)


def _rope_tables(seq, group, rot):
    half = rot // 2
    pos = jnp.arange(seq, dtype=F32)
    inv_freq = ROPE_THETA ** (-jnp.arange(half, dtype=F32) * 2.0 / rot)
    ang = pos[:, None] * inv_freq[None, :]
    cos, sin = jnp.cos(ang), jnp.sin(ang)
    pad = group - rot
    ones = jnp.ones((seq, pad), F32)
    zeros_h = jnp.zeros((seq, half), F32)
    zeros_p = jnp.zeros((seq, pad), F32)
    c = jnp.concatenate([cos, cos, ones], axis=1)
    up = jnp.concatenate([-sin, zeros_h, zeros_p], axis=1)
    dn = jnp.concatenate([zeros_h, sin, zeros_p], axis=1)
    reps = LANES // group
    return tuple(jnp.tile(t, (1, reps)) for t in (c, up, dn))


def _pack_w_in(w):
    d = w.shape[0]
    widths = (512, 512, 512, 512, 128, 128, 512, 64, 8, C_HEADS * C_QK, C_RANK, C_ROPE,
              N_BRANCH * d)
    cuts, acc = [], 0
    for wd in widths[:-1]:
        acc += wd
        cuts.append(acc)
    (a_q, a_k, a_v, b_q, b_k, b_v, i_q, i_k, i_w, c_q, c_kv, c_kr, gates) = jnp.split(w, cuts, axis=1)
    c_q = c_q.reshape(d, C_HEADS, C_QK)
    c_qr = c_q[:, :, :C_ROPE].reshape(d, C_HEADS * C_ROPE)
    c_qn = c_q[:, :, C_ROPE:].reshape(d, C_HEADS * C_NOPE)
    i_w = jnp.pad(i_w, ((0, 0), (0, LANES - I_HEADS)))
    packed = jnp.concatenate([a_q, a_k, a_v, b_q, i_q, c_kv, c_qn, c_qr, b_k, b_v, i_k, c_kr,
                              i_w, gates], axis=1)
    return packed.astype(BF16)


def _pack_w_kv(w):
    r = w.shape[0]
    w = w.reshape(r, C_HEADS, 2, C_NOPE).transpose(0, 2, 1, 3)
    return w.reshape(r, 2 * C_HEADS * C_NOPE).astype(BF16)


def _row(v):
    return v.reshape(1, -1).astype(F32)


def _tile_sizes(s):
    big = 1024 if s % 1024 == 0 else 512
    return dict(tm=big, t_attn=512, tq_b=256, tk_b=512, ts=512)


def kernel(x, attn_norm, w_in, a_q_norm, a_k_norm, a_lambda, a_sub_norm, b_q_norm, b_k_norm,
           idx_k_norm, c_q_norm, c_kv_norm, w_kv_b, c_k_norm, w_branch, w_out, ffn_norm,
           w_gate_up, w_down):
    bsz, seq, d = x.shape
    assert bsz == 1 and seq % 512 == 0
    depth = w_in.shape[0]
    ts = _tile_sizes(seq)
    top_k = min(TOPK_MAX, seq // 4)
    d_ff = w_down.shape[1]

    tab_a = _rope_tables(seq, A_QK, A_QK // 4)
    tab_b = _rope_tables(seq, B_DIM, B_DIM // 4)
    tab_c = _rope_tables(seq, C_ROPE, C_ROPE)
    zeros64 = jnp.zeros((64,), F32)

    h = x.reshape(seq, d)
    for l in range(depth):
        lam_init = 0.8 - 0.6 * math.exp(-0.3 * l)
        proj = _norm_matmul(h, _row(attn_norm[l]), _pack_w_in(w_in[l]), x_col_block=0, k=d,
                            tm=ts["tm"], tn=1024, out_dtype=F32, name="proj_in")
        kv = _norm_matmul(proj, _row(c_kv_norm[l]), _pack_w_kv(w_kv_b[l]),
                          x_col_block=COL_C_KV // C_RANK, k=C_RANK,
                          tm=ts["tm"], tn=1024, out_dtype=F32, name="kv_up")

        aq, ak, av = _prep_a(proj, _row(jnp.tile(a_q_norm[l], 2)), _row(jnp.tile(a_k_norm[l], 2)),
                             tab_a, ts=ts["ts"])
        bq, bk, bv, iq, ik2, iw = _prep_b(
            proj, _row(b_q_norm[l]), _row(b_k_norm[l]),
            _row(jnp.concatenate([idx_k_norm[l], zeros64])), tab_b, tab_a, ts=ts["ts"])
        cq, ck, cv = _prep_c(
            proj, kv, _row(c_q_norm[l][C_ROPE:]), _row(jnp.tile(c_q_norm[l][:C_ROPE], 2)),
            _row(c_k_norm[l][C_ROPE:]), _row(jnp.concatenate([c_k_norm[l][:C_ROPE], zeros64])),
            tab_c, ts=ts["ts"])

        a_o = _attn_a(aq, ak, av, a_lambda[l].astype(F32), _row(a_sub_norm[l]),
                      t=ts["t_attn"], lam_init=lam_init)
        b_o = _attn_b(iq, iw, ik2, bq, bk, bv, tq=ts["tq_b"], tk=ts["tk_b"], top_k=top_k)
        c_o = _attn_c(cq, ck, cv, t=ts["t_attn"])

        merged = _merge(a_o, b_o, c_o, w_branch[l].astype(BF16), proj, d=d, tm=512, tn=1024)
        h = _matmul_residual(merged, w_out[l].astype(BF16), h, tm=ts["tm"], tn=1024, name="out_proj")

        hff = _ffn_up(h, _row(ffn_norm[l]), w_gate_up[l].astype(BF16), tm=ts["tm"], tn=512)
        h = _matmul_residual(hff, w_down[l].astype(BF16), h, tm=512, tn=512, name="ffn_down")
    return h.reshape(bsz, seq, d)
```

```python
import math
from functools import partial

import jax
import jax.numpy as jnp
from jax import lax
from jax.experimental import pallas as pl
from jax.experimental.pallas import tpu as pltpu

F32 = jnp.float32
BF16 = jnp.bfloat16
I32 = jnp.int32

ROPE_THETA = 500000.0
NORM_EPS = 1e-6
LOG2E = math.log2(math.e)
LANES = 128
VMEM_LIMIT = 56 * 1024 * 1024

A_HEADS, A_QK, A_V = 4, 64, 128
B_HEADS, B_DIM = 4, 128
I_HEADS, I_DIM, TOPK_MAX = 8, 64, 256
C_HEADS, C_NOPE, C_ROPE, C_V, C_RANK = 8, 128, 64, 128, 512
C_QK = C_NOPE + C_ROPE
N_BRANCH = 3
INT_MIN = -2 ** 31

COL_A_Q, COL_A_K, COL_A_V, COL_B_Q, COL_I_Q = 0, 512, 1024, 1536, 2048
COL_C_KV, COL_C_QN, COL_C_QR = 2560, 3072, 4096
COL_B_K, COL_B_V, COL_IKR, COL_IW, COL_G = 4608, 4736, 4864, 4992, 5120


def _params(sem):
    return pltpu.CompilerParams(dimension_semantics=sem, vmem_limit_bytes=VMEM_LIMIT)


def _dot(a, b):
    return jnp.dot(a, b, preferred_element_type=F32)


def _dot_nt(a, b):
    return lax.dot_general(a, b, (((1,), (1,)), ((), ())), preferred_element_type=F32)


def _rms_rows(x, g):
    ms = jnp.mean(x * x, axis=-1, keepdims=True)
    return x * lax.rsqrt(ms + NORM_EPS) * g


def _norm_matmul_kernel(x_ref, g_ref, w_ref, o_ref, xn_ref):
    @pl.when(pl.program_id(1) == 0)
    def _():
        xn_ref[...] = _rms_rows(x_ref[...].astype(F32), g_ref[...]).astype(BF16)

    o_ref[...] = _dot(xn_ref[...], w_ref[...]).astype(o_ref.dtype)


def _norm_matmul(x, g, w, *, x_col_block, k, tm, tn, out_dtype, name):
    s = x.shape[0]
    n = w.shape[1]
    return pl.pallas_call(
        _norm_matmul_kernel,
        grid=(s // tm, n // tn),
        in_specs=[
            pl.BlockSpec((tm, k), lambda i, j: (i, x_col_block)),
            pl.BlockSpec((1, k), lambda i, j: (0, 0)),
            pl.BlockSpec((k, tn), lambda i, j: (0, j)),
        ],
        out_specs=pl.BlockSpec((tm, tn), lambda i, j: (i, j)),
        out_shape=jax.ShapeDtypeStruct((s, n), out_dtype),
        scratch_shapes=[pltpu.VMEM((tm, k), BF16)],
        compiler_params=_params(("arbitrary", "arbitrary")),
        name=name,
    )(x, g, w)


def _ffn_up_kernel(x_ref, g_ref, wg_ref, wu_ref, o_ref, xn_ref):
    @pl.when(pl.program_id(1) == 0)
    def _():
        xn_ref[...] = _rms_rows(x_ref[...], g_ref[...]).astype(BF16)

    xn = xn_ref[...]
    gate = _dot(xn, wg_ref[...])
    up = _dot(xn, wu_ref[...])
    o_ref[...] = (gate * (1.0 / (1.0 + jnp.exp(-gate))) * up).astype(o_ref.dtype)


def _ffn_up(x, g, w_gate_up, *, tm, tn):
    s, d = x.shape
    d_ff = w_gate_up.shape[1] // 2
    nj = d_ff // tn
    return pl.pallas_call(
        _ffn_up_kernel,
        grid=(s // tm, nj),
        in_specs=[
            pl.BlockSpec((tm, d), lambda i, j: (i, 0)),
            pl.BlockSpec((1, d), lambda i, j: (0, 0)),
            pl.BlockSpec((d, tn), lambda i, j: (0, j)),
            pl.BlockSpec((d, tn), lambda i, j: (0, j + nj)),
        ],
        out_specs=pl.BlockSpec((tm, tn), lambda i, j: (i, j)),
        out_shape=jax.ShapeDtypeStruct((s, d_ff), BF16),
        scratch_shapes=[pltpu.VMEM((tm, d), BF16)],
        compiler_params=_params(("arbitrary", "arbitrary")),
        name="ffn_up",
    )(x, g, w_gate_up, w_gate_up)


def _matmul_residual_kernel(a_ref, w_ref, r_ref, o_ref):
    o_ref[...] = r_ref[...] + _dot(a_ref[...], w_ref[...])


def _matmul_residual(a, w, res, *, tm, tn, name):
    s, k = a.shape
    n = w.shape[1]
    return pl.pallas_call(
        _matmul_residual_kernel,
        grid=(s // tm, n // tn),
        in_specs=[
            pl.BlockSpec((tm, k), lambda i, j: (i, 0)),
            pl.BlockSpec((k, tn), lambda i, j: (0, j)),
            pl.BlockSpec((tm, tn), lambda i, j: (i, j)),
        ],
        out_specs=pl.BlockSpec((tm, tn), lambda i, j: (i, j)),
        out_shape=jax.ShapeDtypeStruct((s, n), F32),
        compiler_params=_params(("arbitrary", "arbitrary")),
        name=name,
    )(a, w, res)


def _sigmoid(x):
    return 1.0 / (1.0 + jnp.exp(-x))


def _merge_kernel(a_ref, b_ref, c_ref, wa_ref, wb_ref, wc_ref, ga_ref, gb_ref, gc_ref, o_ref):
    merged = _sigmoid(ga_ref[...]) * _dot(a_ref[...], wa_ref[...])
    merged += _sigmoid(gb_ref[...]) * _dot(b_ref[...], wb_ref[...])
    merged += _sigmoid(gc_ref[...]) * _dot(c_ref[...], wc_ref[...])
    o_ref[...] = merged.astype(o_ref.dtype)


def _merge(a_o, b_o, c_o, w_branch, proj, *, d, tm, tn):
    s = a_o.shape[0]
    a_w, b_w, c_w = a_o.shape[1], b_o.shape[1], c_o.shape[1]
    assert a_w == b_w and c_w == a_w + b_w
    g0 = COL_G // tn
    gstep = d // tn
    return pl.pallas_call(
        _merge_kernel,
        grid=(s // tm, d // tn),
        in_specs=[
            pl.BlockSpec((tm, a_w), lambda i, j: (i, 0)),
            pl.BlockSpec((tm, b_w), lambda i, j: (i, 0)),
            pl.BlockSpec((tm, c_w), lambda i, j: (i, 0)),
            pl.BlockSpec((a_w, tn), lambda i, j: (0, j)),
            pl.BlockSpec((b_w, tn), lambda i, j: (1, j)),
            pl.BlockSpec((c_w, tn), lambda i, j: (1, j)),
            pl.BlockSpec((tm, tn), lambda i, j: (i, g0 + j)),
            pl.BlockSpec((tm, tn), lambda i, j: (i, g0 + gstep + j)),
            pl.BlockSpec((tm, tn), lambda i, j: (i, g0 + 2 * gstep + j)),
        ],
        out_specs=pl.BlockSpec((tm, tn), lambda i, j: (i, j)),
        out_shape=jax.ShapeDtypeStruct((s, d), BF16),
        compiler_params=_params(("arbitrary", "arbitrary")),
        name="merge",
    )(a_o, b_o, c_o, w_branch, w_branch, w_branch, proj, proj, proj)


def _lane(shape):
    return lax.broadcasted_iota(I32, shape, 1)


def _half_sums(sq):
    low = _lane(sq.shape) < 64
    lo = jnp.sum(jnp.where(low, sq, 0.0), axis=-1, keepdims=True)
    hi = jnp.sum(jnp.where(low, 0.0, sq), axis=-1, keepdims=True)
    return lo, hi


def _norm64(x, g):
    lo, hi = _half_sums(x * x)
    ms = jnp.where(_lane(x.shape) < 64, lo, hi) * (1.0 / 64.0)
    return x * lax.rsqrt(ms + NORM_EPS) * g


def _norm128(x, g):
    ms = jnp.mean(x * x, axis=-1, keepdims=True)
    return x * lax.rsqrt(ms + NORM_EPS) * g


def _rope(y, cos, sin_up, sin_dn, half):
    return (y * cos + pltpu.roll(y, LANES - half, 1) * sin_up
            + pltpu.roll(y, half, 1) * sin_dn)


def _chunk(ref, c):
    return ref[:, c * LANES:(c + 1) * LANES]


def _prep_a_kernel(q_ref, k_ref, v_ref, gq_ref, gk_ref, cos_ref, su_ref, sd_ref,
                   oq_ref, ok_ref, ov_ref, *, q_scale):
    cos, su, sd = cos_ref[...], su_ref[...], sd_ref[...]
    half = A_QK // 8
    for c in range(A_HEADS):
        sl = slice(c * LANES, (c + 1) * LANES)
        q = _rope(_norm64(_chunk(q_ref, c), gq_ref[...]), cos, su, sd, half)
        k = _rope(_norm64(_chunk(k_ref, c), gk_ref[...]), cos, su, sd, half)
        oq_ref[:, sl] = (q * q_scale).astype(BF16)
        ok_ref[:, sl] = k.astype(BF16)
    ov_ref[...] = v_ref[...].astype(BF16)


def _prep_b_kernel(q_ref, k_ref, v_ref, iq_ref, ikr_ref, iw_ref,
                   gq_ref, gk_ref, gik_ref,
                   cb_ref, sub_ref, sdb_ref, ca_ref, sua_ref, sda_ref,
                   oq_ref, ok_ref, ov_ref, oiq_ref, oik_ref, oiw_ref, *, q_scale, w_scale):
    cb, sub, sdb = cb_ref[...], sub_ref[...], sdb_ref[...]
    ca, sua, sda = ca_ref[...], sua_ref[...], sda_ref[...]
    b_half = B_DIM // 8
    i_half = I_DIM // 8
    for c in range(B_HEADS):
        sl = slice(c * LANES, (c + 1) * LANES)
        q = _rope(_norm128(_chunk(q_ref, c), gq_ref[...]), cb, sub, sdb, b_half)
        oq_ref[:, sl] = (q * q_scale).astype(BF16)
        oiq_ref[:, sl] = _rope(_chunk(iq_ref, c), ca, sua, sda, i_half).astype(BF16)
    ok_ref[...] = _rope(_norm128(k_ref[...], gk_ref[...]), cb, sub, sdb, b_half).astype(BF16)
    ov_ref[...] = v_ref[...].astype(BF16)
    x = ikr_ref[...]
    low = _lane(x.shape) < 64
    ik = _rope(_norm64(x, gik_ref[...]), ca, sua, sda, i_half)
    ik = jnp.where(low, ik, 0.0)
    oik_ref[...] = (ik + pltpu.roll(ik, 64, 1)).astype(BF16)
    oiw_ref[...] = iw_ref[...] * w_scale


def _prep_c_kernel(qn_ref, qr_ref, ikr_ref, kv_ref,
                   gqn_ref, gqr_ref, gkn_ref, gkr_ref, cos_ref, su_ref, sd_ref,
                   oq_ref, ok_ref, ov_ref, *, q_scale):
    cos, su, sd = cos_ref[...], su_ref[...], sd_ref[...]
    half = C_ROPE // 2
    low = _lane(cos.shape) < 64
    inv_d = 1.0 / C_QK

    for c in range(C_HEADS // 2):
        r = _chunk(qr_ref, c)
        lo, hi = _half_sums(r * r)
        rr = _rope(r * gqr_ref[...], cos, su, sd, half)
        parts = (jnp.where(low, rr, 0.0), pltpu.roll(jnp.where(low, 0.0, rr), 64, 1))
        for j, (ss_r, rope_part) in enumerate(zip((lo, hi), parts)):
            h = 2 * c + j
            n = _chunk(qn_ref, h)
            ss = jnp.sum(n * n, axis=-1, keepdims=True) + ss_r
            sc = lax.rsqrt(ss * inv_d + NORM_EPS) * q_scale
            oq_ref[:, 2 * h * LANES:(2 * h + 1) * LANES] = (n * sc * gqn_ref[...]).astype(BF16)
            oq_ref[:, (2 * h + 1) * LANES:(2 * h + 2) * LANES] = (rope_part * sc).astype(BF16)

    x = ikr_ref[...]
    kr = pltpu.roll(jnp.where(_lane(x.shape) < 64, 0.0, x), 64, 1)
    ss_r = jnp.sum(kr * kr, axis=-1, keepdims=True)
    kr = _rope(kr * gkr_ref[...], cos, su, sd, half)
    for h in range(C_HEADS):
        n = _chunk(kv_ref, h)
        ss = jnp.sum(n * n, axis=-1, keepdims=True) + ss_r
        sc = lax.rsqrt(ss * inv_d + NORM_EPS)
        ok_ref[:, 2 * h * LANES:(2 * h + 1) * LANES] = (n * sc * gkn_ref[...]).astype(BF16)
        ok_ref[:, (2 * h + 1) * LANES:(2 * h + 2) * LANES] = (kr * sc).astype(BF16)
    v0 = C_HEADS * C_NOPE
    ov_ref[...] = kv_ref[:, v0:v0 + C_HEADS * C_V].astype(BF16)


def _row_spec(ts, width, col_block):
    return pl.BlockSpec((ts, width), lambda i: (i, col_block))


def _const_spec(shape):
    return pl.BlockSpec(shape, lambda i: (0,) * len(shape))


def _prep_a(proj, gq, gk, tab, *, ts):
    s = proj.shape[0]
    w = A_HEADS * 2 * A_QK
    tspec = _row_spec(ts, LANES, 0)
    out = jax.ShapeDtypeStruct((s, w), BF16)
    return pl.pallas_call(
        partial(_prep_a_kernel, q_scale=A_QK ** -0.5 * LOG2E),
        grid=(s // ts,),
        in_specs=[_row_spec(ts, w, COL_A_Q // w), _row_spec(ts, w, COL_A_K // w),
                  _row_spec(ts, w, COL_A_V // w),
                  _const_spec((1, LANES)), _const_spec((1, LANES)), tspec, tspec, tspec],
        out_specs=[_row_spec(ts, w, 0)] * 3,
        out_shape=[out, out, out],
        compiler_params=_params(("arbitrary",)),
        name="prep_a",
    )(proj, proj, proj, gq, gk, *tab)


def _prep_b(proj, gq, gk, gik, tab_b, tab_a, *, ts):
    s = proj.shape[0]
    w = B_HEADS * B_DIM
    tspec = _row_spec(ts, LANES, 0)
    gspec = _const_spec((1, LANES))
    wide = jax.ShapeDtypeStruct((s, w), BF16)
    narrow = jax.ShapeDtypeStruct((s, LANES), BF16)
    return pl.pallas_call(
        partial(_prep_b_kernel, q_scale=B_DIM ** -0.5 * LOG2E,
                w_scale=I_HEADS ** -0.5 * I_DIM ** -0.5),
        grid=(s // ts,),
        in_specs=[_row_spec(ts, w, COL_B_Q // w), _row_spec(ts, LANES, COL_B_K // LANES),
                  _row_spec(ts, LANES, COL_B_V // LANES), _row_spec(ts, w, COL_I_Q // w),
                  _row_spec(ts, LANES, COL_IKR // LANES), _row_spec(ts, LANES, COL_IW // LANES),
                  gspec, gspec, gspec] + [tspec] * 6,
        out_specs=[_row_spec(ts, w, 0), tspec, tspec, _row_spec(ts, w, 0), tspec, tspec],
        out_shape=[wide, narrow, narrow, wide, narrow,
                   jax.ShapeDtypeStruct((s, LANES), F32)],
        compiler_params=_params(("arbitrary",)),
        name="prep_b",
    )(proj, proj, proj, proj, proj, proj, gq, gk, gik, *tab_b, *tab_a)


def _prep_c(proj, kv, gqn, gqr, gkn, gkr, tab, *, ts):
    s = proj.shape[0]
    wn, wr = C_HEADS * C_NOPE, C_HEADS * C_ROPE
    wqk = C_HEADS * 2 * LANES
    tspec = _row_spec(ts, LANES, 0)
    gspec = _const_spec((1, LANES))
    return pl.pallas_call(
        partial(_prep_c_kernel, q_scale=C_QK ** -0.5 * LOG2E),
        grid=(s // ts,),
        in_specs=[_row_spec(ts, wn, COL_C_QN // wn), _row_spec(ts, wr, COL_C_QR // wr),
                  _row_spec(ts, LANES, COL_IKR // LANES), _row_spec(ts, kv.shape[1], 0),
                  gspec, gspec, gspec, gspec, tspec, tspec, tspec],
        out_specs=[_row_spec(ts, wqk, 0), _row_spec(ts, wqk, 0), _row_spec(ts, C_HEADS * C_V, 0)],
        out_shape=[jax.ShapeDtypeStruct((s, wqk), BF16), jax.ShapeDtypeStruct((s, wqk), BF16),
                   jax.ShapeDtypeStruct((s, C_HEADS * C_V), BF16)],
        compiler_params=_params(("arbitrary",)),
        name="prep_c",
    )(proj, proj, proj, kv, gqn, gqr, gkn, gkr, *tab)


def _flash_step(q, kc, vc, carry, mask):
    m, l, acc = carry
    s = _dot_nt(q, kc)
    if mask is not None:
        s = jnp.where(mask, s, -jnp.inf)
    m_new = jnp.maximum(m, jnp.max(s, axis=-1, keepdims=True))
    alpha = jnp.exp2(m - m_new)
    p = jnp.exp2(s - m_new)
    l = alpha * l + jnp.sum(p, axis=-1, keepdims=True)
    acc = alpha * acc + _dot(p.astype(BF16), vc)
    return m_new, l, acc


def _flash_init(rows, dv, m0):
    return (jnp.full((rows, 1), m0, F32), jnp.zeros((rows, 1), F32), jnp.zeros((rows, dv), F32))


def _causal_flash(q, k_ref, v_ref, i, t, reps):
    rows = reps * t

    def body(c, carry):
        return _flash_step(q, k_ref[c], v_ref[c], carry, None)

    carry = lax.fori_loop(0, i, body, _flash_init(rows, v_ref.shape[-1], -jnp.inf))
    assert reps in (1, 2)
    row = lax.broadcasted_iota(I32, (rows, t), 0)
    if reps == 2:
        row = jnp.where(row >= t, row - t, row)
    col = lax.broadcasted_iota(I32, (rows, t), 1)
    return _flash_step(q, k_ref[i], v_ref[i], carry, col <= row)


def _attn_c_kernel(q_ref, k_ref, v_ref, o_ref, *, t):
    _, l, acc = _causal_flash(q_ref[...], k_ref, v_ref, pl.program_id(1), t, 1)
    o_ref[...] = (acc / l).astype(o_ref.dtype)


def _attn_c(cq, ck, cv, *, t):
    s = cq.shape[0]
    nc = s // t
    dq = 2 * LANES
    ck3 = ck.reshape(nc, t, ck.shape[1])
    cv3 = cv.reshape(nc, t, cv.shape[1])
    return pl.pallas_call(
        partial(_attn_c_kernel, t=t),
        grid=(C_HEADS, nc),
        in_specs=[pl.BlockSpec((t, dq), lambda h, i: (i, h)),
                  pl.BlockSpec((nc, t, dq), lambda h, i: (0, 0, h)),
                  pl.BlockSpec((nc, t, C_V), lambda h, i: (0, 0, h))],
        out_specs=pl.BlockSpec((t, C_V), lambda h, i: (i, h)),
        out_shape=jax.ShapeDtypeStruct((s, C_HEADS * C_V), BF16),
        compiler_params=_params(("arbitrary", "arbitrary")),
        name="attn_c",
    )(cq, ck3, cv3)


def _attn_a_kernel(lam_ref, g_ref, q_ref, k_ref, v_ref, o_ref, *, t, lam_init):
    q = q_ref[...]
    low = _lane(q.shape) < A_QK
    zero = jnp.zeros_like(q)
    qq = jnp.concatenate([jnp.where(low, q, zero), jnp.where(low, zero, q)], axis=0)
    _, l, acc = _causal_flash(qq, k_ref, v_ref, pl.program_id(1), t, 2)
    lf = lam_ref[...]
    lam = (jnp.exp(jnp.sum(lf[0:1, :] * lf[1:2, :], axis=-1, keepdims=True))
           - jnp.exp(jnp.sum(lf[2:3, :] * lf[3:4, :], axis=-1, keepdims=True)) + lam_init)
    o = acc[:t] / l[:t] - lam * (acc[t:] / l[t:])
    o_ref[...] = (_norm128(o, g_ref[...]) * (1.0 - lam_init)).astype(o_ref.dtype)


def _attn_a(aq, ak, av, a_lambda, g_sub, *, t, lam_init):
    s = aq.shape[0]
    nc = s // t
    ak3 = ak.reshape(nc, t, ak.shape[1])
    av3 = av.reshape(nc, t, av.shape[1])
    return pl.pallas_call(
        partial(_attn_a_kernel, t=t, lam_init=lam_init),
        grid=(A_HEADS, nc),
        in_specs=[pl.BlockSpec(a_lambda.shape, lambda h, i: (0, 0)),
                  pl.BlockSpec((1, LANES), lambda h, i: (0, 0)),
                  pl.BlockSpec((t, LANES), lambda h, i: (i, h)),
                  pl.BlockSpec((nc, t, LANES), lambda h, i: (0, 0, h)),
                  pl.BlockSpec((nc, t, A_V), lambda h, i: (0, 0, h))],
        out_specs=pl.BlockSpec((t, A_V), lambda h, i: (i, h)),
        out_shape=jax.ShapeDtypeStruct((s, A_HEADS * A_V), BF16),
        compiler_params=_params(("arbitrary", "arbitrary")),
        name="attn_a",
    )(a_lambda, g_sub, aq, ak3, av3)


def _sortable_key(score):
    bits = lax.bitcast_convert_type(jnp.where(score == 0.0, 0.0, score), I32)
    return bits ^ (lax.shift_right_arithmetic(bits, 31) & 0x7FFFFFFF)


ROW_BLOCK = 64


def _attn_b_kernel(iq_ref, iw_ref, ik_ref, q_ref, k_ref, v_ref, o_ref,
                   key_ref, iqs_ref, qs_ref, *, tq, tk, top_k):
    i = pl.program_id(0)
    nfull = (i * tq) // tk
    nchunks = nfull + 1
    dv = v_ref.shape[-1]
    neg = -1e30

    iq = iq_ref[...]
    low = _lane((tq, LANES)) < I_DIM
    for j in range(I_HEADS // 2):
        blk = iq[:, j * LANES:(j + 1) * LANES]
        zero = jnp.zeros_like(blk)
        iqs_ref[(2 * j) * tq:(2 * j + 1) * tq, :] = jnp.where(low, blk, zero)
        iqs_ref[(2 * j + 1) * tq:(2 * j + 2) * tq, :] = jnp.where(low, zero, blk)
    for h in range(B_HEADS):
        qs_ref[h * tq:(h + 1) * tq, :] = q_ref[:, h * B_DIM:(h + 1) * B_DIM]
    iw = iw_ref[...]
    ws = [iw[:, h:h + 1] for h in range(I_HEADS)]

    def score_chunk(c):
        s = _dot_nt(iqs_ref[...], ik_ref[c])
        score = ws[0] * jnp.maximum(s[0:tq], 0.0)
        for h in range(1, I_HEADS):
            score += ws[h] * jnp.maximum(s[h * tq:(h + 1) * tq], 0.0)
        return _sortable_key(score)

    def score_body(c, _):
        key_ref[c] = score_chunk(c)
        return 0

    lax.fori_loop(0, nfull, score_body, 0)
    row_g = i * tq + lax.broadcasted_iota(I32, (tq, tk), 0)
    col_g = nfull * tk + lax.broadcasted_iota(I32, (tq, tk), 1)
    key_ref[nfull] = jnp.where(col_g <= row_g, score_chunk(nfull), INT_MIN)

    def count_block(r0, pred):
        def body(c, acc):
            key = key_ref[c, r0:r0 + ROW_BLOCK, :]
            for g in range(tk // LANES):
                acc += jnp.where(pred(key[:, g * LANES:(g + 1) * LANES]), 1, 0)
            return acc
        acc = lax.fori_loop(0, nchunks, body, jnp.zeros((ROW_BLOCK, LANES), I32))
        return jnp.sum(acc, axis=-1, keepdims=True)

    def lanes(col):
        return jnp.broadcast_to(col, (ROW_BLOCK, LANES))

    thr_parts, cnt_parts = [], []
    for r0 in range(0, tq, ROW_BLOCK):
        c0 = count_block(r0, lambda k: k >= 0)
        ok0 = c0 >= top_k
        thr0 = jnp.where(ok0, 0, INT_MIN).astype(I32)
        cnt0 = jnp.where(ok0, c0, nchunks * tk)

        def bisect(b, carry, r0=r0):
            thr, cnt = carry
            cand = thr + lax.shift_left(jnp.int32(1), 30 - b)
            cand_l = lanes(cand)
            c = count_block(r0, lambda k: k >= cand_l)
            ok = c >= top_k
            return jnp.where(ok, cand, thr), jnp.where(ok, c, cnt)

        thr_b, cnt_b = lax.fori_loop(0, 31, bisect, (thr0, cnt0))
        thr_parts.append(thr_b)
        cnt_parts.append(cnt_b)
    thr = jnp.concatenate(thr_parts, axis=0)
    cnt_ge = jnp.concatenate(cnt_parts, axis=0)
    thr = jnp.maximum(thr, INT_MIN + 1)
    rows = i * tq + lax.broadcasted_iota(I32, (tq, 1), 0)
    has_ties = jnp.max(jnp.where(rows < top_k, 0, cnt_ge - top_k)) > 0

    qs = qs_ref[...]
    nrow = B_HEADS * tq

    def flash(c, sel, state):
        m, l, acc = state
        s = _dot_nt(qs, k_ref[c])
        s = jnp.concatenate([jnp.where(sel, s[h * tq:(h + 1) * tq], neg) for h in range(B_HEADS)],
                            axis=0)
        m_new = jnp.maximum(m, jnp.max(s, axis=-1, keepdims=True))
        alpha = jnp.exp2(m - m_new)
        p = jnp.exp2(s - m_new)
        l = alpha * l + jnp.sum(p, axis=-1, keepdims=True)
        acc = alpha * acc + _dot(p.astype(BF16), v_ref[c])
        return m_new, l, acc

    init = _flash_init(nrow, dv, neg)

    def run_exact_count():
        def body(c, state):
            return flash(c, key_ref[c] >= thr, state)
        return lax.fori_loop(0, nchunks, body, init)

    def run_with_ties():
        cnt_gt = jnp.concatenate(
            [count_block(r0, lambda k, t=lanes(thr[r0:r0 + ROW_BLOCK]): k > t)
             for r0 in range(0, tq, ROW_BLOCK)], axis=0)
        need = (top_k - cnt_gt).astype(F32)
        tri = (lax.broadcasted_iota(I32, (tk, tk), 0)
               <= lax.broadcasted_iota(I32, (tk, tk), 1)).astype(BF16)

        def body(c, carry):
            seen, state = carry
            key = key_ref[c]
            tie = key == thr
            tie_f = jnp.where(tie, 1.0, 0.0)
            rank = seen + _dot(tie_f.astype(BF16), tri)
            sel = (key > thr) | (tie & (rank <= need))
            seen = seen + jnp.sum(tie_f, axis=-1, keepdims=True)
            return seen, flash(c, sel, state)

        _, state = lax.fori_loop(0, nchunks, body, (jnp.zeros((tq, 1), F32), init))
        return state

    _, l, acc = lax.cond(has_ties, run_with_ties, run_exact_count)
    out = acc / l
    for h in range(B_HEADS):
        o_ref[:, h * dv:(h + 1) * dv] = out[h * tq:(h + 1) * tq].astype(o_ref.dtype)


def _attn_b(iq, iw, ik2, bq, bk, bv, *, tq, tk, top_k):
    s = bq.shape[0]
    assert tk % tq == 0 and tq % ROW_BLOCK == 0 and tk >= top_k
    nc = s // tk
    ik3 = ik2.reshape(nc, tk, LANES)
    bk3 = bk.reshape(nc, tk, B_DIM)
    bv3 = bv.reshape(nc, tk, B_DIM)
    w = B_HEADS * B_DIM
    whole = lambda shape: pl.BlockSpec(shape, lambda i: (0, 0, 0))
    return pl.pallas_call(
        partial(_attn_b_kernel, tq=tq, tk=tk, top_k=top_k),
        grid=(s // tq,),
        in_specs=[pl.BlockSpec((tq, I_HEADS * I_DIM), lambda i: (i, 0)),
                  pl.BlockSpec((tq, LANES), lambda i: (i, 0)),
                  whole((nc, tk, LANES)),
                  pl.BlockSpec((tq, w), lambda i: (i, 0)),
                  whole((nc, tk, B_DIM)), whole((nc, tk, B_DIM))],
        out_specs=pl.BlockSpec((tq, w), lambda i: (i, 0)),
        out_shape=jax.ShapeDtypeStruct((s, w), BF16),
        scratch_shapes=[pltpu.VMEM((nc, tq, tk), I32),
                        pltpu.VMEM((I_HEADS * tq, LANES), BF16),
                        pltpu.VMEM((B_HEADS * tq, B_DIM), BF16)],
        compiler_params=_params(("arbitrary",)),
        name="attn_b",
    )(iq, iw, ik3, bq, bk3, bv3)


def _rope_tables(seq, group, rot):
    half = rot // 2
    pos = jnp.arange(seq, dtype=F32)
    inv_freq = ROPE_THETA ** (-jnp.arange(half, dtype=F32) * 2.0 / rot)
    ang = pos[:, None] * inv_freq[None, :]
    cos, sin = jnp.cos(ang), jnp.sin(ang)
    pad = group - rot
    ones = jnp.ones((seq, pad), F32)
    zeros_h = jnp.zeros((seq, half), F32)
    zeros_p = jnp.zeros((seq, pad), F32)
    c = jnp.concatenate([cos, cos, ones], axis=1)
    up = jnp.concatenate([-sin, zeros_h, zeros_p], axis=1)
    dn = jnp.concatenate([zeros_h, sin, zeros_p], axis=1)
    reps = LANES // group
    return tuple(jnp.tile(t, (1, reps)) for t in (c, up, dn))


def _pack_w_in(w):
    d = w.shape[0]
    widths = (512, 512, 512, 512, 128, 128, 512, 64, 8, C_HEADS * C_QK, C_RANK, C_ROPE,
              N_BRANCH * d)
    cuts, acc = [], 0
    for wd in widths[:-1]:
        acc += wd
        cuts.append(acc)
    (a_q, a_k, a_v, b_q, b_k, b_v, i_q, i_k, i_w, c_q, c_kv, c_kr, gates) = jnp.split(w, cuts, axis=1)
    c_q = c_q.reshape(d, C_HEADS, C_QK)
    c_qr = c_q[:, :, :C_ROPE].reshape(d, C_HEADS * C_ROPE)
    c_qn = c_q[:, :, C_ROPE:].reshape(d, C_HEADS * C_NOPE)
    i_w = jnp.pad(i_w, ((0, 0), (0, LANES - I_HEADS)))
    packed = jnp.concatenate([a_q, a_k, a_v, b_q, i_q, c_kv, c_qn, c_qr, b_k, b_v, i_k, c_kr,
                              i_w, gates], axis=1)
    return packed.astype(BF16)


def _pack_w_kv(w):
    r = w.shape[0]
    w = w.reshape(r, C_HEADS, 2, C_NOPE).transpose(0, 2, 1, 3)
    return w.reshape(r, 2 * C_HEADS * C_NOPE).astype(BF16)


def _row(v):
    return v.reshape(1, -1).astype(F32)


def _tile_sizes(s):
    big = 1024 if s % 1024 == 0 else 512
    return dict(tm=big, t_attn=big, tq_b=256, tk_b=big, ts=512)


def kernel(x, attn_norm, w_in, a_q_norm, a_k_norm, a_lambda, a_sub_norm, b_q_norm, b_k_norm,
           idx_k_norm, c_q_norm, c_kv_norm, w_kv_b, c_k_norm, w_branch, w_out, ffn_norm,
           w_gate_up, w_down):
    bsz, seq, d = x.shape
    assert bsz == 1 and seq % 512 == 0
    depth = w_in.shape[0]
    ts = _tile_sizes(seq)
    top_k = min(TOPK_MAX, seq // 4)
    d_ff = w_down.shape[1]

    tab_a = _rope_tables(seq, A_QK, A_QK // 4)
    tab_b = _rope_tables(seq, B_DIM, B_DIM // 4)
    tab_c = _rope_tables(seq, C_ROPE, C_ROPE)
    zeros64 = jnp.zeros((64,), F32)

    h = x.reshape(seq, d)
    for l in range(depth):
        lam_init = 0.8 - 0.6 * math.exp(-0.3 * l)
        proj = _norm_matmul(h, _row(attn_norm[l]), _pack_w_in(w_in[l]), x_col_block=0, k=d,
                            tm=ts["tm"], tn=1024, out_dtype=F32, name="proj_in")
        kv = _norm_matmul(proj, _row(c_kv_norm[l]), _pack_w_kv(w_kv_b[l]),
                          x_col_block=COL_C_KV // C_RANK, k=C_RANK,
                          tm=ts["tm"], tn=1024, out_dtype=F32, name="kv_up")

        aq, ak, av = _prep_a(proj, _row(jnp.tile(a_q_norm[l], 2)), _row(jnp.tile(a_k_norm[l], 2)),
                             tab_a, ts=ts["ts"])
        bq, bk, bv, iq, ik2, iw = _prep_b(
            proj, _row(b_q_norm[l]), _row(b_k_norm[l]),
            _row(jnp.concatenate([idx_k_norm[l], zeros64])), tab_b, tab_a, ts=ts["ts"])
        cq, ck, cv = _prep_c(
            proj, kv, _row(c_q_norm[l][C_ROPE:]), _row(jnp.tile(c_q_norm[l][:C_ROPE], 2)),
            _row(c_k_norm[l][C_ROPE:]), _row(jnp.concatenate([c_k_norm[l][:C_ROPE], zeros64])),
            tab_c, ts=ts["ts"])

        a_o = _attn_a(aq, ak, av, a_lambda[l].astype(F32), _row(a_sub_norm[l]),
                      t=ts["t_attn"], lam_init=lam_init)
        b_o = _attn_b(iq, iw, ik2, bq, bk, bv, tq=ts["tq_b"], tk=ts["tk_b"], top_k=top_k)
        c_o = _attn_c(cq, ck, cv, t=ts["t_attn"])

        merged = _merge(a_o, b_o, c_o, w_branch[l].astype(BF16), proj, d=d, tm=512, tn=1024)
        h = _matmul_residual(merged, w_out[l].astype(BF16), h, tm=ts["tm"], tn=1024, name="out_proj")

        hff = _ffn_up(h, _row(ffn_norm[l]), w_gate_up[l].astype(BF16), tm=ts["tm"], tn=512)
        h = _matmul_residual(hff, w_down[l].astype(BF16), h, tm=512, tn=512, name="ffn_down")
    return h.reshape(bsz, seq, d)
```

```python
import math
from functools import partial

import jax
import jax.numpy as jnp
from jax import lax
from jax.experimental import pallas as pl
from jax.experimental.pallas import tpu as pltpu

F32 = jnp.float32
BF16 = jnp.bfloat16
I32 = jnp.int32

ROPE_THETA = 500000.0
NORM_EPS = 1e-6
LOG2E = math.log2(math.e)
LANES = 128
VMEM_LIMIT = 56 * 1024 * 1024

A_HEADS, A_QK, A_V = 4, 64, 128
B_HEADS, B_DIM = 4, 128
I_HEADS, I_DIM, TOPK_MAX = 8, 64, 256
C_HEADS, C_NOPE, C_ROPE, C_V, C_RANK = 8, 128, 64, 128, 512
C_QK = C_NOPE + C_ROPE
N_BRANCH = 3
INT_MIN = -2 ** 31

COL_A_Q, COL_A_K, COL_A_V, COL_B_Q, COL_I_Q = 0, 512, 1024, 1536, 2048
COL_C_KV, COL_C_QN, COL_C_QR = 2560, 3072, 4096
COL_B_K, COL_B_V, COL_IKR, COL_IW, COL_G = 4608, 4736, 4864, 4992, 5120


def _params(sem):
    return pltpu.CompilerParams(dimension_semantics=sem, vmem_limit_bytes=VMEM_LIMIT)


def _dot(a, b):
    return jnp.dot(a, b, preferred_element_type=F32)


def _dot_nt(a, b):
    return lax.dot_general(a, b, (((1,), (1,)), ((), ())), preferred_element_type=F32)


def _rms_rows(x, g):
    ms = jnp.mean(x * x, axis=-1, keepdims=True)
    return x * lax.rsqrt(ms + NORM_EPS) * g


def _norm_matmul_kernel(x_ref, g_ref, w_ref, o_ref, xn_ref):
    @pl.when(pl.program_id(1) == 0)
    def _():
        xn_ref[...] = _rms_rows(x_ref[...].astype(F32), g_ref[...]).astype(BF16)

    o_ref[...] = _dot(xn_ref[...], w_ref[...]).astype(o_ref.dtype)


def _norm_matmul(x, g, w, *, x_col_block, k, tm, tn, out_dtype, name):
    s = x.shape[0]
    n = w.shape[1]
    return pl.pallas_call(
        _norm_matmul_kernel,
        grid=(s // tm, n // tn),
        in_specs=[
            pl.BlockSpec((tm, k), lambda i, j: (i, x_col_block)),
            pl.BlockSpec((1, k), lambda i, j: (0, 0)),
            pl.BlockSpec((k, tn), lambda i, j: (0, j)),
        ],
        out_specs=pl.BlockSpec((tm, tn), lambda i, j: (i, j)),
        out_shape=jax.ShapeDtypeStruct((s, n), out_dtype),
        scratch_shapes=[pltpu.VMEM((tm, k), BF16)],
        compiler_params=_params(("arbitrary", "arbitrary")),
        name=name,
    )(x, g, w)


def _ffn_up_kernel(x_ref, g_ref, wg_ref, wu_ref, o_ref, xn_ref):
    @pl.when(pl.program_id(1) == 0)
    def _():
        xn_ref[...] = _rms_rows(x_ref[...], g_ref[...]).astype(BF16)

    xn = xn_ref[...]
    gate = _dot(xn, wg_ref[...])
    up = _dot(xn, wu_ref[...])
    o_ref[...] = (gate * (1.0 / (1.0 + jnp.exp(-gate))) * up).astype(o_ref.dtype)


def _ffn_up(x, g, w_gate_up, *, tm, tn):
    s, d = x.shape
    d_ff = w_gate_up.shape[1] // 2
    nj = d_ff // tn
    return pl.pallas_call(
        _ffn_up_kernel,
        grid=(s // tm, nj),
        in_specs=[
            pl.BlockSpec((tm, d), lambda i, j: (i, 0)),
            pl.BlockSpec((1, d), lambda i, j: (0, 0)),
            pl.BlockSpec((d, tn), lambda i, j: (0, j)),
            pl.BlockSpec((d, tn), lambda i, j: (0, j + nj)),
        ],
        out_specs=pl.BlockSpec((tm, tn), lambda i, j: (i, j)),
        out_shape=jax.ShapeDtypeStruct((s, d_ff), BF16),
        scratch_shapes=[pltpu.VMEM((tm, d), BF16)],
        compiler_params=_params(("arbitrary", "arbitrary")),
        name="ffn_up",
    )(x, g, w_gate_up, w_gate_up)


def _matmul_residual_kernel(a_ref, w_ref, r_ref, o_ref):
    o_ref[...] = r_ref[...] + _dot(a_ref[...], w_ref[...])


def _matmul_residual(a, w, res, *, tm, tn, name):
    s, k = a.shape
    n = w.shape[1]
    return pl.pallas_call(
        _matmul_residual_kernel,
        grid=(s // tm, n // tn),
        in_specs=[
            pl.BlockSpec((tm, k), lambda i, j: (i, 0)),
            pl.BlockSpec((k, tn), lambda i, j: (0, j)),
            pl.BlockSpec((tm, tn), lambda i, j: (i, j)),
        ],
        out_specs=pl.BlockSpec((tm, tn), lambda i, j: (i, j)),
        out_shape=jax.ShapeDtypeStruct((s, n), F32),
        compiler_params=_params(("arbitrary", "arbitrary")),
        name=name,
    )(a, w, res)


def _sigmoid(x):
    return 1.0 / (1.0 + jnp.exp(-x))


def _merge_kernel(a_ref, b_ref, c_ref, wa_ref, wb_ref, wc_ref, ga_ref, gb_ref, gc_ref, o_ref):
    merged = _sigmoid(ga_ref[...]) * _dot(a_ref[...], wa_ref[...])
    merged += _sigmoid(gb_ref[...]) * _dot(b_ref[...], wb_ref[...])
    merged += _sigmoid(gc_ref[...]) * _dot(c_ref[...], wc_ref[...])
    o_ref[...] = merged.astype(o_ref.dtype)


def _merge(a_o, b_o, c_o, w_branch, proj, *, d, tm, tn):
    s = a_o.shape[0]
    a_w, b_w, c_w = a_o.shape[1], b_o.shape[1], c_o.shape[1]
    assert a_w == b_w and c_w == a_w + b_w
    g0 = COL_G // tn
    gstep = d // tn
    return pl.pallas_call(
        _merge_kernel,
        grid=(s // tm, d // tn),
        in_specs=[
            pl.BlockSpec((tm, a_w), lambda i, j: (i, 0)),
            pl.BlockSpec((tm, b_w), lambda i, j: (i, 0)),
            pl.BlockSpec((tm, c_w), lambda i, j: (i, 0)),
            pl.BlockSpec((a_w, tn), lambda i, j: (0, j)),
            pl.BlockSpec((b_w, tn), lambda i, j: (1, j)),
            pl.BlockSpec((c_w, tn), lambda i, j: (1, j)),
            pl.BlockSpec((tm, tn), lambda i, j: (i, g0 + j)),
            pl.BlockSpec((tm, tn), lambda i, j: (i, g0 + gstep + j)),
            pl.BlockSpec((tm, tn), lambda i, j: (i, g0 + 2 * gstep + j)),
        ],
        out_specs=pl.BlockSpec((tm, tn), lambda i, j: (i, j)),
        out_shape=jax.ShapeDtypeStruct((s, d), BF16),
        compiler_params=_params(("arbitrary", "arbitrary")),
        name="merge",
    )(a_o, b_o, c_o, w_branch, w_branch, w_branch, proj, proj, proj)


def _lane(shape):
    return lax.broadcasted_iota(I32, shape, 1)


def _half_sums(sq):
    low = _lane(sq.shape) < 64
    lo = jnp.sum(jnp.where(low, sq, 0.0), axis=-1, keepdims=True)
    hi = jnp.sum(jnp.where(low, 0.0, sq), axis=-1, keepdims=True)
    return lo, hi


def _norm64(x, g):
    lo, hi = _half_sums(x * x)
    ms = jnp.where(_lane(x.shape) < 64, lo, hi) * (1.0 / 64.0)
    return x * lax.rsqrt(ms + NORM_EPS) * g


def _norm128(x, g):
    ms = jnp.mean(x * x, axis=-1, keepdims=True)
    return x * lax.rsqrt(ms + NORM_EPS) * g


def _rope(y, cos, sin_up, sin_dn, half):
    return (y * cos + pltpu.roll(y, LANES - half, 1) * sin_up
            + pltpu.roll(y, half, 1) * sin_dn)


def _chunk(ref, c):
    return ref[:, c * LANES:(c + 1) * LANES]


def _prep_a_kernel(q_ref, k_ref, v_ref, gq_ref, gk_ref, cos_ref, su_ref, sd_ref,
                   oq_ref, ok_ref, ov_ref, *, q_scale):
    cos, su, sd = cos_ref[...], su_ref[...], sd_ref[...]
    half = A_QK // 8
    for c in range(A_HEADS):
        sl = slice(c * LANES, (c + 1) * LANES)
        q = _rope(_norm64(_chunk(q_ref, c), gq_ref[...]), cos, su, sd, half)
        k = _rope(_norm64(_chunk(k_ref, c), gk_ref[...]), cos, su, sd, half)
        oq_ref[:, sl] = (q * q_scale).astype(BF16)
        ok_ref[:, sl] = k.astype(BF16)
    ov_ref[...] = v_ref[...].astype(BF16)


def _prep_b_kernel(q_ref, k_ref, v_ref, iq_ref, ikr_ref, iw_ref,
                   gq_ref, gk_ref, gik_ref,
                   cb_ref, sub_ref, sdb_ref, ca_ref, sua_ref, sda_ref,
                   oq_ref, ok_ref, ov_ref, oiq_ref, oik_ref, oiw_ref, *, q_scale, w_scale):
    cb, sub, sdb = cb_ref[...], sub_ref[...], sdb_ref[...]
    ca, sua, sda = ca_ref[...], sua_ref[...], sda_ref[...]
    b_half = B_DIM // 8
    i_half = I_DIM // 8
    for c in range(B_HEADS):
        sl = slice(c * LANES, (c + 1) * LANES)
        q = _rope(_norm128(_chunk(q_ref, c), gq_ref[...]), cb, sub, sdb, b_half)
        oq_ref[:, sl] = (q * q_scale).astype(BF16)
        oiq_ref[:, sl] = _rope(_chunk(iq_ref, c), ca, sua, sda, i_half).astype(BF16)
    ok_ref[...] = _rope(_norm128(k_ref[...], gk_ref[...]), cb, sub, sdb, b_half).astype(BF16)
    ov_ref[...] = v_ref[...].astype(BF16)
    x = ikr_ref[...]
    low = _lane(x.shape) < 64
    ik = _rope(_norm64(x, gik_ref[...]), ca, sua, sda, i_half)
    ik = jnp.where(low, ik, 0.0)
    oik_ref[...] = (ik + pltpu.roll(ik, 64, 1)).astype(BF16)
    oiw_ref[...] = iw_ref[...] * w_scale


def _prep_c_kernel(qn_ref, qr_ref, ikr_ref, kv_ref,
                   gqn_ref, gqr_ref, gkn_ref, gkr_ref, cos_ref, su_ref, sd_ref,
                   oq_ref, ok_ref, ov_ref, *, q_scale):
    cos, su, sd = cos_ref[...], su_ref[...], sd_ref[...]
    half = C_ROPE // 2
    low = _lane(cos.shape) < 64
    inv_d = 1.0 / C_QK

    for c in range(C_HEADS // 2):
        r = _chunk(qr_ref, c)
        lo, hi = _half_sums(r * r)
        rr = _rope(r * gqr_ref[...], cos, su, sd, half)
        parts = (jnp.where(low, rr, 0.0), pltpu.roll(jnp.where(low, 0.0, rr), 64, 1))
        for j, (ss_r, rope_part) in enumerate(zip((lo, hi), parts)):
            h = 2 * c + j
            n = _chunk(qn_ref, h)
            ss = jnp.sum(n * n, axis=-1, keepdims=True) + ss_r
            sc = lax.rsqrt(ss * inv_d + NORM_EPS) * q_scale
            oq_ref[:, 2 * h * LANES:(2 * h + 1) * LANES] = (n * sc * gqn_ref[...]).astype(BF16)
            oq_ref[:, (2 * h + 1) * LANES:(2 * h + 2) * LANES] = (rope_part * sc).astype(BF16)

    x = ikr_ref[...]
    kr = pltpu.roll(jnp.where(_lane(x.shape) < 64, 0.0, x), 64, 1)
    ss_r = jnp.sum(kr * kr, axis=-1, keepdims=True)
    kr = _rope(kr * gkr_ref[...], cos, su, sd, half)
    for h in range(C_HEADS):
        n = _chunk(kv_ref, 2 * h)
        ss = jnp.sum(n * n, axis=-1, keepdims=True) + ss_r
        sc = lax.rsqrt(ss * inv_d + NORM_EPS)
        ok_ref[:, 2 * h * LANES:(2 * h + 1) * LANES] = (n * sc * gkn_ref[...]).astype(BF16)
        ok_ref[:, (2 * h + 1) * LANES:(2 * h + 2) * LANES] = (kr * sc).astype(BF16)
        ov_ref[:, h * C_V:(h + 1) * C_V] = _chunk(kv_ref, 2 * h + 1).astype(BF16)


def _row_spec(ts, width, col_block):
    return pl.BlockSpec((ts, width), lambda i: (i, col_block))


def _const_spec(shape):
    return pl.BlockSpec(shape, lambda i: (0,) * len(shape))


def _prep_a(proj, gq, gk, tab, *, ts):
    s = proj.shape[0]
    w = A_HEADS * 2 * A_QK
    tspec = _row_spec(ts, LANES, 0)
    out = jax.ShapeDtypeStruct((s, w), BF16)
    return pl.pallas_call(
        partial(_prep_a_kernel, q_scale=A_QK ** -0.5 * LOG2E),
        grid=(s // ts,),
        in_specs=[_row_spec(ts, w, COL_A_Q // w), _row_spec(ts, w, COL_A_K // w),
                  _row_spec(ts, w, COL_A_V // w),
                  _const_spec((1, LANES)), _const_spec((1, LANES)), tspec, tspec, tspec],
        out_specs=[_row_spec(ts, w, 0)] * 3,
        out_shape=[out, out, out],
        compiler_params=_params(("arbitrary",)),
        name="prep_a",
    )(proj, proj, proj, gq, gk, *tab)


def _prep_b(proj, gq, gk, gik, tab_b, tab_a, *, ts):
    s = proj.shape[0]
    w = B_HEADS * B_DIM
    tspec = _row_spec(ts, LANES, 0)
    gspec = _const_spec((1, LANES))
    wide = jax.ShapeDtypeStruct((s, w), BF16)
    narrow = jax.ShapeDtypeStruct((s, LANES), BF16)
    return pl.pallas_call(
        partial(_prep_b_kernel, q_scale=B_DIM ** -0.5 * LOG2E,
                w_scale=I_HEADS ** -0.5 * I_DIM ** -0.5),
        grid=(s // ts,),
        in_specs=[_row_spec(ts, w, COL_B_Q // w), _row_spec(ts, LANES, COL_B_K // LANES),
                  _row_spec(ts, LANES, COL_B_V // LANES), _row_spec(ts, w, COL_I_Q // w),
                  _row_spec(ts, LANES, COL_IKR // LANES), _row_spec(ts, LANES, COL_IW // LANES),
                  gspec, gspec, gspec] + [tspec] * 6,
        out_specs=[_row_spec(ts, w, 0), tspec, tspec, _row_spec(ts, w, 0), tspec, tspec],
        out_shape=[wide, narrow, narrow, wide, narrow,
                   jax.ShapeDtypeStruct((s, LANES), F32)],
        compiler_params=_params(("arbitrary",)),
        name="prep_b",
    )(proj, proj, proj, proj, proj, proj, gq, gk, gik, *tab_b, *tab_a)


def _prep_c(proj, kv, gqn, gqr, gkn, gkr, tab, *, ts):
    s = proj.shape[0]
    wn, wr = C_HEADS * C_NOPE, C_HEADS * C_ROPE
    wqk = C_HEADS * 2 * LANES
    tspec = _row_spec(ts, LANES, 0)
    gspec = _const_spec((1, LANES))
    return pl.pallas_call(
        partial(_prep_c_kernel, q_scale=C_QK ** -0.5 * LOG2E),
        grid=(s // ts,),
        in_specs=[_row_spec(ts, wn, COL_C_QN // wn), _row_spec(ts, wr, COL_C_QR // wr),
                  _row_spec(ts, LANES, COL_IKR // LANES), _row_spec(ts, kv.shape[1], 0),
                  gspec, gspec, gspec, gspec, tspec, tspec, tspec],
        out_specs=[_row_spec(ts, wqk, 0), _row_spec(ts, wqk, 0), _row_spec(ts, C_HEADS * C_V, 0)],
        out_shape=[jax.ShapeDtypeStruct((s, wqk), BF16), jax.ShapeDtypeStruct((s, wqk), BF16),
                   jax.ShapeDtypeStruct((s, C_HEADS * C_V), BF16)],
        compiler_params=_params(("arbitrary",)),
        name="prep_c",
    )(proj, proj, proj, kv, gqn, gqr, gkn, gkr, *tab)


def _flash_step(q, kc, vc, carry, mask):
    m, l, acc = carry
    s = _dot_nt(q, kc)
    if mask is not None:
        s = jnp.where(mask, s, -jnp.inf)
    m_new = jnp.maximum(m, jnp.max(s, axis=-1, keepdims=True))
    alpha = jnp.exp2(m - m_new)
    p = jnp.exp2(s - m_new)
    l = alpha * l + jnp.sum(p, axis=-1, keepdims=True)
    acc = alpha * acc + _dot(p.astype(BF16), vc)
    return m_new, l, acc


def _flash_init(rows, dv, m0):
    return (jnp.full((rows, 1), m0, F32), jnp.zeros((rows, 1), F32), jnp.zeros((rows, dv), F32))


def _causal_flash(q, k_ref, v_ref, i, t, reps):
    rows = reps * t

    def body(c, carry):
        return _flash_step(q, k_ref[c], v_ref[c], carry, None)

    carry = lax.fori_loop(0, i, body, _flash_init(rows, v_ref.shape[-1], -jnp.inf))
    assert reps in (1, 2)
    row = lax.broadcasted_iota(I32, (rows, t), 0)
    if reps == 2:
        row = jnp.where(row >= t, row - t, row)
    col = lax.broadcasted_iota(I32, (rows, t), 1)
    return _flash_step(q, k_ref[i], v_ref[i], carry, col <= row)


def _attn_c_kernel(q_ref, k_ref, v_ref, o_ref, *, t):
    _, l, acc = _causal_flash(q_ref[...], k_ref, v_ref, pl.program_id(1), t, 1)
    o_ref[...] = (acc / l).astype(o_ref.dtype)


def _attn_c(cq, ck, cv, *, t):
    s = cq.shape[0]
    nc = s // t
    dq = 2 * LANES
    ck3 = ck.reshape(nc, t, ck.shape[1])
    cv3 = cv.reshape(nc, t, cv.shape[1])
    return pl.pallas_call(
        partial(_attn_c_kernel, t=t),
        grid=(C_HEADS, nc),
        in_specs=[pl.BlockSpec((t, dq), lambda h, i: (i, h)),
                  pl.BlockSpec((nc, t, dq), lambda h, i: (0, 0, h)),
                  pl.BlockSpec((nc, t, C_V), lambda h, i: (0, 0, h))],
        out_specs=pl.BlockSpec((t, C_V), lambda h, i: (i, h)),
        out_shape=jax.ShapeDtypeStruct((s, C_HEADS * C_V), BF16),
        compiler_params=_params(("arbitrary", "arbitrary")),
        name="attn_c",
    )(cq, ck3, cv3)


def _attn_a_kernel(lam_ref, g_ref, q_ref, k_ref, v_ref, o_ref, *, t, lam_init):
    q = q_ref[...]
    low = _lane(q.shape) < A_QK
    zero = jnp.zeros_like(q)
    qq = jnp.concatenate([jnp.where(low, q, zero), jnp.where(low, zero, q)], axis=0)
    _, l, acc = _causal_flash(qq, k_ref, v_ref, pl.program_id(1), t, 2)
    lf = lam_ref[...]
    lam = (jnp.exp(jnp.sum(lf[0:1, :] * lf[1:2, :], axis=-1, keepdims=True))
           - jnp.exp(jnp.sum(lf[2:3, :] * lf[3:4, :], axis=-1, keepdims=True)) + lam_init)
    o = acc[:t] / l[:t] - lam * (acc[t:] / l[t:])
    o_ref[...] = (_norm128(o, g_ref[...]) * (1.0 - lam_init)).astype(o_ref.dtype)


def _attn_a(aq, ak, av, a_lambda, g_sub, *, t, lam_init):
    s = aq.shape[0]
    nc = s // t
    ak3 = ak.reshape(nc, t, ak.shape[1])
    av3 = av.reshape(nc, t, av.shape[1])
    return pl.pallas_call(
        partial(_attn_a_kernel, t=t, lam_init=lam_init),
        grid=(A_HEADS, nc),
        in_specs=[pl.BlockSpec(a_lambda.shape, lambda h, i: (0, 0)),
                  pl.BlockSpec((1, LANES), lambda h, i: (0, 0)),
                  pl.BlockSpec((t, LANES), lambda h, i: (i, h)),
                  pl.BlockSpec((nc, t, LANES), lambda h, i: (0, 0, h)),
                  pl.BlockSpec((nc, t, A_V), lambda h, i: (0, 0, h))],
        out_specs=pl.BlockSpec((t, A_V), lambda h, i: (i, h)),
        out_shape=jax.ShapeDtypeStruct((s, A_HEADS * A_V), BF16),
        compiler_params=_params(("arbitrary", "arbitrary")),
        name="attn_a",
    )(a_lambda, g_sub, aq, ak3, av3)


def _sortable_key(score):
    bits = lax.bitcast_convert_type(jnp.where(score == 0.0, 0.0, score), I32)
    return bits ^ (lax.shift_right_arithmetic(bits, 31) & 0x7FFFFFFF)


ROW_BLOCK = 128


def _attn_b_kernel(iq_ref, iw_ref, ik_ref, q_ref, k_ref, v_ref, o_ref,
                   key_ref, iqs_ref, qs_ref, *, tq, tk, top_k):
    i = pl.program_id(0)
    nfull = (i * tq) // tk
    nchunks = nfull + 1
    dv = v_ref.shape[-1]
    neg = -1e30

    iq = iq_ref[...]
    low = _lane((tq, LANES)) < I_DIM
    for j in range(I_HEADS // 2):
        blk = iq[:, j * LANES:(j + 1) * LANES]
        zero = jnp.zeros_like(blk)
        iqs_ref[(2 * j) * tq:(2 * j + 1) * tq, :] = jnp.where(low, blk, zero)
        iqs_ref[(2 * j + 1) * tq:(2 * j + 2) * tq, :] = jnp.where(low, zero, blk)
    for h in range(B_HEADS):
        qs_ref[h * tq:(h + 1) * tq, :] = q_ref[:, h * B_DIM:(h + 1) * B_DIM]
    iw = iw_ref[...]
    ws = [iw[:, h:h + 1] for h in range(I_HEADS)]

    def score_chunk(c):
        s = _dot_nt(iqs_ref[...], ik_ref[c])
        score = ws[0] * jnp.maximum(s[0:tq], 0.0)
        for h in range(1, I_HEADS):
            score += ws[h] * jnp.maximum(s[h * tq:(h + 1) * tq], 0.0)
        return _sortable_key(score)

    def score_body(c, _):
        key_ref[c] = score_chunk(c)
        return 0

    lax.fori_loop(0, nfull, score_body, 0)
    row_g = i * tq + lax.broadcasted_iota(I32, (tq, tk), 0)
    col_g = nfull * tk + lax.broadcasted_iota(I32, (tq, tk), 1)
    key_ref[nfull] = jnp.where(col_g <= row_g, score_chunk(nfull), INT_MIN)

    ones_sq = jnp.ones((LANES, LANES), BF16)

    def count(compare, level):
        accs = []
        for r0 in range(0, tq, ROW_BLOCK):
            lvl = level[r0:r0 + ROW_BLOCK]

            def body(c, acc, r0=r0, lvl=lvl):
                key = key_ref[c, r0:r0 + ROW_BLOCK, :]
                for g in range(tk // LANES):
                    acc += jnp.where(compare(key[:, g * LANES:(g + 1) * LANES], lvl), 1, 0)
                return acc

            accs.append(lax.fori_loop(0, nchunks, body, jnp.zeros((ROW_BLOCK, LANES), I32)))
        return _dot(jnp.concatenate(accs, axis=0).astype(F32).astype(BF16), ones_sq)

    c0 = count(lambda k, t: k >= t, jnp.zeros((tq, LANES), I32))
    ok0 = c0 >= top_k
    thr0 = jnp.where(ok0, 0, INT_MIN).astype(I32)
    cnt0 = jnp.where(ok0, c0, (nchunks * tk).astype(F32))

    def bisect(b, carry):
        thr, cnt = carry
        cand = thr + lax.shift_left(jnp.int32(1), 30 - b)
        c = count(lambda k, t: k >= t, cand)
        ok = c >= top_k
        return jnp.where(ok, cand, thr), jnp.where(ok, c, cnt)

    thr_l, cnt_l = lax.fori_loop(0, 31, bisect, (thr0, cnt0))
    thr_l = jnp.maximum(thr_l, INT_MIN + 1)
    thr = thr_l[:, 0:1]
    rows = i * tq + lax.broadcasted_iota(I32, (tq, 1), 0)
    has_ties = jnp.max(jnp.where(rows < top_k, 0.0, cnt_l[:, 0:1] - top_k)) > 0.0

    qs = qs_ref[...]
    nrow = B_HEADS * tq

    def flash(c, sel, state):
        m, l, acc = state
        s = _dot_nt(qs, k_ref[c])
        s = jnp.concatenate([jnp.where(sel, s[h * tq:(h + 1) * tq], neg) for h in range(B_HEADS)],
                            axis=0)
        m_new = jnp.maximum(m, jnp.max(s, axis=-1, keepdims=True))
        alpha = jnp.exp2(m - m_new)
        p = jnp.exp2(s - m_new)
        l = alpha * l + jnp.sum(p, axis=-1, keepdims=True)
        acc = alpha * acc + _dot(p.astype(BF16), v_ref[c])
        return m_new, l, acc

    init = _flash_init(nrow, dv, neg)

    def run_exact_count():
        def body(c, state):
            return flash(c, key_ref[c] >= thr, state)
        return lax.fori_loop(0, nchunks, body, init)

    def run_with_ties():
        need = top_k - count(lambda k, t: k > t, thr_l)[:, 0:1]
        tri = (lax.broadcasted_iota(I32, (tk, tk), 0)
               <= lax.broadcasted_iota(I32, (tk, tk), 1)).astype(BF16)

        def body(c, carry):
            seen, state = carry
            key = key_ref[c]
            tie = key == thr
            tie_f = jnp.where(tie, 1.0, 0.0)
            rank = seen + _dot(tie_f.astype(BF16), tri)
            sel = (key > thr) | (tie & (rank <= need))
            seen = seen + jnp.sum(tie_f, axis=-1, keepdims=True)
            return seen, flash(c, sel, state)

        _, state = lax.fori_loop(0, nchunks, body, (jnp.zeros((tq, 1), F32), init))
        return state

    _, l, acc = lax.cond(has_ties, run_with_ties, run_exact_count)
    out = acc / l
    for h in range(B_HEADS):
        o_ref[:, h * dv:(h + 1) * dv] = out[h * tq:(h + 1) * tq].astype(o_ref.dtype)


def _attn_b(iq, iw, ik2, bq, bk, bv, *, tq, tk, top_k):
    s = bq.shape[0]
    assert tk % tq == 0 and tq % ROW_BLOCK == 0 and tk >= top_k and s // LANES <= 256
    nc = s // tk
    ik3 = ik2.reshape(nc, tk, LANES)
    bk3 = bk.reshape(nc, tk, B_DIM)
    bv3 = bv.reshape(nc, tk, B_DIM)
    w = B_HEADS * B_DIM
    whole = lambda shape: pl.BlockSpec(shape, lambda i: (0, 0, 0))
    return pl.pallas_call(
        partial(_attn_b_kernel, tq=tq, tk=tk, top_k=top_k),
        grid=(s // tq,),
        in_specs=[pl.BlockSpec((tq, I_HEADS * I_DIM), lambda i: (i, 0)),
                  pl.BlockSpec((tq, LANES), lambda i: (i, 0)),
                  whole((nc, tk, LANES)),
                  pl.BlockSpec((tq, w), lambda i: (i, 0)),
                  whole((nc, tk, B_DIM)), whole((nc, tk, B_DIM))],
        out_specs=pl.BlockSpec((tq, w), lambda i: (i, 0)),
        out_shape=jax.ShapeDtypeStruct((s, w), BF16),
        scratch_shapes=[pltpu.VMEM((nc, tq, tk), I32),
                        pltpu.VMEM((I_HEADS * tq, LANES), BF16),
                        pltpu.VMEM((B_HEADS * tq, B_DIM), BF16)],
        compiler_params=_params(("arbitrary",)),
        name="attn_b",
    )(iq, iw, ik3, bq, bk3, bv3)


def _rope_tables(seq, group, rot):
    half = rot // 2
    pos = jnp.arange(seq, dtype=F32)
    inv_freq = ROPE_THETA ** (-jnp.arange(half, dtype=F32) * 2.0 / rot)
    ang = pos[:, None] * inv_freq[None, :]
    cos, sin = jnp.cos(ang), jnp.sin(ang)
    pad = group - rot
    ones = jnp.ones((seq, pad), F32)
    zeros_h = jnp.zeros((seq, half), F32)
    zeros_p = jnp.zeros((seq, pad), F32)
    c = jnp.concatenate([cos, cos, ones], axis=1)
    up = jnp.concatenate([-sin, zeros_h, zeros_p], axis=1)
    dn = jnp.concatenate([zeros_h, sin, zeros_p], axis=1)
    reps = LANES // group
    return tuple(jnp.tile(t, (1, reps)) for t in (c, up, dn))


def _pack_w_in(w):
    d = w.shape[0]
    widths = (512, 512, 512, 512, 128, 128, 512, 64, 8, C_HEADS * C_QK, C_RANK, C_ROPE,
              N_BRANCH * d)
    cuts, acc = [], 0
    for wd in widths[:-1]:
        acc += wd
        cuts.append(acc)
    edges = [0] + cuts + [w.shape[1]]
    (a_q, a_k, a_v, b_q, b_k, b_v, i_q, i_k, i_w, c_q, c_kv, c_kr, gates) = [
        w[:, lo:hi].astype(BF16) for lo, hi in zip(edges[:-1], edges[1:])]
    c_q = c_q.reshape(d, C_HEADS, C_QK)
    c_qr = c_q[:, :, :C_ROPE].reshape(d, C_HEADS * C_ROPE)
    c_qn = c_q[:, :, C_ROPE:].reshape(d, C_HEADS * C_NOPE)
    i_w = jnp.pad(i_w, ((0, 0), (0, LANES - I_HEADS)))
    return jnp.concatenate([a_q, a_k, a_v, b_q, i_q, c_kv, c_qn, c_qr, b_k, b_v, i_k, c_kr,
                            i_w, gates], axis=1)


def _row(v):
    return v.reshape(1, -1).astype(F32)


def _tile_sizes(s):
    big = 1024 if s % 1024 == 0 else 512
    return dict(tm=big, t_attn=big, tq_b=256, tk_b=big, ts=512)


def kernel(x, attn_norm, w_in, a_q_norm, a_k_norm, a_lambda, a_sub_norm, b_q_norm, b_k_norm,
           idx_k_norm, c_q_norm, c_kv_norm, w_kv_b, c_k_norm, w_branch, w_out, ffn_norm,
           w_gate_up, w_down):
    bsz, seq, d = x.shape
    assert bsz == 1 and seq % 512 == 0
    depth = w_in.shape[0]
    ts = _tile_sizes(seq)
    top_k = min(TOPK_MAX, seq // 4)
    d_ff = w_down.shape[1]

    tab_a = _rope_tables(seq, A_QK, A_QK // 4)
    tab_b = _rope_tables(seq, B_DIM, B_DIM // 4)
    tab_c = _rope_tables(seq, C_ROPE, C_ROPE)
    zeros64 = jnp.zeros((64,), F32)

    h = x.reshape(seq, d)
    for l in range(depth):
        lam_init = 0.8 - 0.6 * math.exp(-0.3 * l)
        proj = _norm_matmul(h, _row(attn_norm[l]), _pack_w_in(w_in[l]), x_col_block=0, k=d,
                            tm=ts["tm"], tn=1024, out_dtype=F32, name="proj_in")
        kv = _norm_matmul(proj, _row(c_kv_norm[l]), w_kv_b[l].astype(BF16),
                          x_col_block=COL_C_KV // C_RANK, k=C_RANK,
                          tm=ts["tm"], tn=1024, out_dtype=F32, name="kv_up")

        aq, ak, av = _prep_a(proj, _row(jnp.tile(a_q_norm[l], 2)), _row(jnp.tile(a_k_norm[l], 2)),
                             tab_a, ts=ts["ts"])
        bq, bk, bv, iq, ik2, iw = _prep_b(
            proj, _row(b_q_norm[l]), _row(b_k_norm[l]),
            _row(jnp.concatenate([idx_k_norm[l], zeros64])), tab_b, tab_a, ts=ts["ts"])
        cq, ck, cv = _prep_c(
            proj, kv, _row(c_q_norm[l][C_ROPE:]), _row(jnp.tile(c_q_norm[l][:C_ROPE], 2)),
            _row(c_k_norm[l][C_ROPE:]), _row(jnp.concatenate([c_k_norm[l][:C_ROPE], zeros64])),
            tab_c, ts=ts["ts"])

        a_o = _attn_a(aq, ak, av, a_lambda[l].astype(F32), _row(a_sub_norm[l]),
                      t=ts["t_attn"], lam_init=lam_init)
        b_o = _attn_b(iq, iw, ik2, bq, bk, bv, tq=ts["tq_b"], tk=ts["tk_b"], top_k=top_k)
        c_o = _attn_c(cq, ck, cv, t=ts["t_attn"])

        merged = _merge(a_o, b_o, c_o, w_branch[l].astype(BF16), proj, d=d, tm=512, tn=1024)
        h = _matmul_residual(merged, w_out[l].astype(BF16), h, tm=ts["tm"], tn=1024, name="out_proj")

        hff = _ffn_up(h, _row(ffn_norm[l]), w_gate_up[l].astype(BF16), tm=ts["tm"], tn=512)
        h = _matmul_residual(hff, w_down[l].astype(BF16), h, tm=512, tn=512, name="ffn_down")
    return h.reshape(bsz, seq, d)
```

```python
import math
from functools import partial

import jax
import jax.numpy as jnp
from jax import lax
from jax.experimental import pallas as pl
from jax.experimental.pallas import tpu as pltpu

F32 = jnp.float32
BF16 = jnp.bfloat16
I32 = jnp.int32

ROPE_THETA = 500000.0
NORM_EPS = 1e-6
LOG2E = math.log2(math.e)
LANES = 128
VMEM_LIMIT = 56 * 1024 * 1024

A_HEADS, A_QK, A_V = 4, 64, 128
B_HEADS, B_DIM = 4, 128
I_HEADS, I_DIM, TOPK_MAX = 8, 64, 256
C_HEADS, C_NOPE, C_ROPE, C_V, C_RANK = 8, 128, 64, 128, 512
C_QK = C_NOPE + C_ROPE
N_BRANCH = 3
INT_MIN = -2 ** 31

COL_A_Q, COL_A_K, COL_A_V, COL_B_Q, COL_I_Q = 0, 512, 1024, 1536, 2048
COL_C_KV, COL_C_QN, COL_C_QR = 2560, 3072, 4096
COL_B_K, COL_B_V, COL_IKR, COL_IW, COL_G = 4608, 4736, 4864, 4992, 5120


def _params(sem):
    return pltpu.CompilerParams(dimension_semantics=sem, vmem_limit_bytes=VMEM_LIMIT)


def _dot(a, b):
    return jnp.dot(a, b, preferred_element_type=F32)


def _dot_nt(a, b):
    return lax.dot_general(a, b, (((1,), (1,)), ((), ())), preferred_element_type=F32)


def _rms_rows(x, g):
    ms = jnp.mean(x * x, axis=-1, keepdims=True)
    return x * lax.rsqrt(ms + NORM_EPS) * g


def _norm_matmul_kernel(x_ref, g_ref, w_ref, o_ref, xn_ref):
    @pl.when(pl.program_id(1) == 0)
    def _():
        xn_ref[...] = _rms_rows(x_ref[...].astype(F32), g_ref[...]).astype(BF16)

    o_ref[...] = _dot(xn_ref[...], w_ref[...]).astype(o_ref.dtype)


def _norm_matmul(x, g, w, *, x_col_block, k, tm, tn, out_dtype, name):
    s = x.shape[0]
    n = w.shape[1]
    return pl.pallas_call(
        _norm_matmul_kernel,
        grid=(s // tm, n // tn),
        in_specs=[
            pl.BlockSpec((tm, k), lambda i, j: (i, x_col_block)),
            pl.BlockSpec((1, k), lambda i, j: (0, 0)),
            pl.BlockSpec((k, tn), lambda i, j: (0, j)),
        ],
        out_specs=pl.BlockSpec((tm, tn), lambda i, j: (i, j)),
        out_shape=jax.ShapeDtypeStruct((s, n), out_dtype),
        scratch_shapes=[pltpu.VMEM((tm, k), BF16)],
        compiler_params=_params(("arbitrary", "arbitrary")),
        name=name,
    )(x, g, w)


def _ffn_up_kernel(x_ref, g_ref, wg_ref, wu_ref, o_ref, xn_ref):
    @pl.when(pl.program_id(1) == 0)
    def _():
        xn_ref[...] = _rms_rows(x_ref[...], g_ref[...]).astype(BF16)

    xn = xn_ref[...]
    gate = _dot(xn, wg_ref[...])
    up = _dot(xn, wu_ref[...])
    o_ref[...] = (gate * (1.0 / (1.0 + jnp.exp(-gate))) * up).astype(o_ref.dtype)


def _ffn_up(x, g, w_gate_up, *, tm, tn):
    s, d = x.shape
    d_ff = w_gate_up.shape[1] // 2
    nj = d_ff // tn
    return pl.pallas_call(
        _ffn_up_kernel,
        grid=(s // tm, nj),
        in_specs=[
            pl.BlockSpec((tm, d), lambda i, j: (i, 0)),
            pl.BlockSpec((1, d), lambda i, j: (0, 0)),
            pl.BlockSpec((d, tn), lambda i, j: (0, j)),
            pl.BlockSpec((d, tn), lambda i, j: (0, j + nj)),
        ],
        out_specs=pl.BlockSpec((tm, tn), lambda i, j: (i, j)),
        out_shape=jax.ShapeDtypeStruct((s, d_ff), BF16),
        scratch_shapes=[pltpu.VMEM((tm, d), BF16)],
        compiler_params=_params(("arbitrary", "arbitrary")),
        name="ffn_up",
    )(x, g, w_gate_up, w_gate_up)


def _matmul_residual_kernel(a_ref, w_ref, r_ref, o_ref):
    o_ref[...] = r_ref[...] + _dot(a_ref[...], w_ref[...])


def _matmul_residual(a, w, res, *, tm, tn, name):
    s, k = a.shape
    n = w.shape[1]
    return pl.pallas_call(
        _matmul_residual_kernel,
        grid=(s // tm, n // tn),
        in_specs=[
            pl.BlockSpec((tm, k), lambda i, j: (i, 0)),
            pl.BlockSpec((k, tn), lambda i, j: (0, j)),
            pl.BlockSpec((tm, tn), lambda i, j: (i, j)),
        ],
        out_specs=pl.BlockSpec((tm, tn), lambda i, j: (i, j)),
        out_shape=jax.ShapeDtypeStruct((s, n), F32),
        compiler_params=_params(("arbitrary", "arbitrary")),
        name=name,
    )(a, w, res)


def _sigmoid(x):
    return 1.0 / (1.0 + jnp.exp(-x))


def _merge_kernel(a_ref, b_ref, c_ref, wa_ref, wb_ref, wc_ref, ga_ref, gb_ref, gc_ref, o_ref):
    merged = _sigmoid(ga_ref[...]) * _dot(a_ref[...], wa_ref[...])
    merged += _sigmoid(gb_ref[...]) * _dot(b_ref[...], wb_ref[...])
    merged += _sigmoid(gc_ref[...]) * _dot(c_ref[...], wc_ref[...])
    o_ref[...] = merged.astype(o_ref.dtype)


def _merge(a_o, b_o, c_o, w_branch, proj, *, d, tm, tn):
    s = a_o.shape[0]
    a_w, b_w, c_w = a_o.shape[1], b_o.shape[1], c_o.shape[1]
    assert a_w == b_w and c_w == a_w + b_w
    g0 = COL_G // tn
    gstep = d // tn
    return pl.pallas_call(
        _merge_kernel,
        grid=(s // tm, d // tn),
        in_specs=[
            pl.BlockSpec((tm, a_w), lambda i, j: (i, 0)),
            pl.BlockSpec((tm, b_w), lambda i, j: (i, 0)),
            pl.BlockSpec((tm, c_w), lambda i, j: (i, 0)),
            pl.BlockSpec((a_w, tn), lambda i, j: (0, j)),
            pl.BlockSpec((b_w, tn), lambda i, j: (1, j)),
            pl.BlockSpec((c_w, tn), lambda i, j: (1, j)),
            pl.BlockSpec((tm, tn), lambda i, j: (i, g0 + j)),
            pl.BlockSpec((tm, tn), lambda i, j: (i, g0 + gstep + j)),
            pl.BlockSpec((tm, tn), lambda i, j: (i, g0 + 2 * gstep + j)),
        ],
        out_specs=pl.BlockSpec((tm, tn), lambda i, j: (i, j)),
        out_shape=jax.ShapeDtypeStruct((s, d), BF16),
        compiler_params=_params(("arbitrary", "arbitrary")),
        name="merge",
    )(a_o, b_o, c_o, w_branch, w_branch, w_branch, proj, proj, proj)


def _lane(shape):
    return lax.broadcasted_iota(I32, shape, 1)


def _half_sums(sq):
    low = _lane(sq.shape) < 64
    lo = jnp.sum(jnp.where(low, sq, 0.0), axis=-1, keepdims=True)
    hi = jnp.sum(jnp.where(low, 0.0, sq), axis=-1, keepdims=True)
    return lo, hi


def _norm64(x, g):
    lo, hi = _half_sums(x * x)
    ms = jnp.where(_lane(x.shape) < 64, lo, hi) * (1.0 / 64.0)
    return x * lax.rsqrt(ms + NORM_EPS) * g


def _norm128(x, g):
    ms = jnp.mean(x * x, axis=-1, keepdims=True)
    return x * lax.rsqrt(ms + NORM_EPS) * g


def _rope(y, cos, sin_up, sin_dn, half):
    return (y * cos + pltpu.roll(y, LANES - half, 1) * sin_up
            + pltpu.roll(y, half, 1) * sin_dn)


def _chunk(ref, c):
    return ref[:, c * LANES:(c + 1) * LANES]


def _prep_a_kernel(q_ref, k_ref, v_ref, gq_ref, gk_ref, cos_ref, su_ref, sd_ref,
                   oq_ref, ok_ref, ov_ref, *, q_scale):
    cos, su, sd = cos_ref[...], su_ref[...], sd_ref[...]
    half = A_QK // 8
    for c in range(A_HEADS):
        sl = slice(c * LANES, (c + 1) * LANES)
        q = _rope(_norm64(_chunk(q_ref, c), gq_ref[...]), cos, su, sd, half)
        k = _rope(_norm64(_chunk(k_ref, c), gk_ref[...]), cos, su, sd, half)
        oq_ref[:, sl] = (q * q_scale).astype(BF16)
        ok_ref[:, sl] = k.astype(BF16)
    ov_ref[...] = v_ref[...].astype(BF16)


def _prep_b_kernel(q_ref, k_ref, v_ref, iq_ref, ikr_ref, iw_ref,
                   gq_ref, gk_ref, gik_ref,
                   cb_ref, sub_ref, sdb_ref, ca_ref, sua_ref, sda_ref,
                   oq_ref, ok_ref, ov_ref, oiq_ref, oik_ref, oiw_ref, *, q_scale, w_scale):
    cb, sub, sdb = cb_ref[...], sub_ref[...], sdb_ref[...]
    ca, sua, sda = ca_ref[...], sua_ref[...], sda_ref[...]
    b_half = B_DIM // 8
    i_half = I_DIM // 8
    for c in range(B_HEADS):
        sl = slice(c * LANES, (c + 1) * LANES)
        q = _rope(_norm128(_chunk(q_ref, c), gq_ref[...]), cb, sub, sdb, b_half)
        oq_ref[:, sl] = (q * q_scale).astype(BF16)
        oiq_ref[:, sl] = _rope(_chunk(iq_ref, c), ca, sua, sda, i_half).astype(BF16)
    ok_ref[...] = _rope(_norm128(k_ref[...], gk_ref[...]), cb, sub, sdb, b_half).astype(BF16)
    ov_ref[...] = v_ref[...].astype(BF16)
    x = ikr_ref[...]
    low = _lane(x.shape) < 64
    ik = _rope(_norm64(x, gik_ref[...]), ca, sua, sda, i_half)
    ik = jnp.where(low, ik, 0.0)
    oik_ref[...] = (ik + pltpu.roll(ik, 64, 1)).astype(BF16)
    oiw_ref[...] = iw_ref[...] * w_scale


def _prep_c_kernel(qn_ref, qr_ref, ikr_ref, kv_ref,
                   gqn_ref, gqr_ref, gkn_ref, gkr_ref, cos_ref, su_ref, sd_ref,
                   oq_ref, ok_ref, ov_ref, *, q_scale):
    cos, su, sd = cos_ref[...], su_ref[...], sd_ref[...]
    half = C_ROPE // 2
    low = _lane(cos.shape) < 64
    inv_d = 1.0 / C_QK

    for c in range(C_HEADS // 2):
        r = _chunk(qr_ref, c)
        lo, hi = _half_sums(r * r)
        rr = _rope(r * gqr_ref[...], cos, su, sd, half)
        parts = (jnp.where(low, rr, 0.0), pltpu.roll(jnp.where(low, 0.0, rr), 64, 1))
        for j, (ss_r, rope_part) in enumerate(zip((lo, hi), parts)):
            h = 2 * c + j
            n = _chunk(qn_ref, h)
            ss = jnp.sum(n * n, axis=-1, keepdims=True) + ss_r
            sc = lax.rsqrt(ss * inv_d + NORM_EPS) * q_scale
            oq_ref[:, 2 * h * LANES:(2 * h + 1) * LANES] = (n * sc * gqn_ref[...]).astype(BF16)
            oq_ref[:, (2 * h + 1) * LANES:(2 * h + 2) * LANES] = (rope_part * sc).astype(BF16)

    x = ikr_ref[...]
    kr = pltpu.roll(jnp.where(_lane(x.shape) < 64, 0.0, x), 64, 1)
    ss_r = jnp.sum(kr * kr, axis=-1, keepdims=True)
    kr = _rope(kr * gkr_ref[...], cos, su, sd, half)
    for h in range(C_HEADS):
        n = _chunk(kv_ref, 2 * h)
        ss = jnp.sum(n * n, axis=-1, keepdims=True) + ss_r
        sc = lax.rsqrt(ss * inv_d + NORM_EPS)
        ok_ref[:, 2 * h * LANES:(2 * h + 1) * LANES] = (n * sc * gkn_ref[...]).astype(BF16)
        ok_ref[:, (2 * h + 1) * LANES:(2 * h + 2) * LANES] = (kr * sc).astype(BF16)
        ov_ref[:, h * C_V:(h + 1) * C_V] = _chunk(kv_ref, 2 * h + 1).astype(BF16)


def _row_spec(ts, width, col_block):
    return pl.BlockSpec((ts, width), lambda i: (i, col_block))


def _const_spec(shape):
    return pl.BlockSpec(shape, lambda i: (0,) * len(shape))


def _prep_a(proj, gq, gk, tab, *, ts):
    s = proj.shape[0]
    w = A_HEADS * 2 * A_QK
    tspec = _row_spec(ts, LANES, 0)
    out = jax.ShapeDtypeStruct((s, w), BF16)
    return pl.pallas_call(
        partial(_prep_a_kernel, q_scale=A_QK ** -0.5 * LOG2E),
        grid=(s // ts,),
        in_specs=[_row_spec(ts, w, COL_A_Q // w), _row_spec(ts, w, COL_A_K // w),
                  _row_spec(ts, w, COL_A_V // w),
                  _const_spec((1, LANES)), _const_spec((1, LANES)), tspec, tspec, tspec],
        out_specs=[_row_spec(ts, w, 0)] * 3,
        out_shape=[out, out, out],
        compiler_params=_params(("arbitrary",)),
        name="prep_a",
    )(proj, proj, proj, gq, gk, *tab)


def _prep_b(proj, gq, gk, gik, tab_b, tab_a, *, ts):
    s = proj.shape[0]
    w = B_HEADS * B_DIM
    tspec = _row_spec(ts, LANES, 0)
    gspec = _const_spec((1, LANES))
    wide = jax.ShapeDtypeStruct((s, w), BF16)
    narrow = jax.ShapeDtypeStruct((s, LANES), BF16)
    return pl.pallas_call(
        partial(_prep_b_kernel, q_scale=B_DIM ** -0.5 * LOG2E,
                w_scale=I_HEADS ** -0.5 * I_DIM ** -0.5),
        grid=(s // ts,),
        in_specs=[_row_spec(ts, w, COL_B_Q // w), _row_spec(ts, LANES, COL_B_K // LANES),
                  _row_spec(ts, LANES, COL_B_V // LANES), _row_spec(ts, w, COL_I_Q // w),
                  _row_spec(ts, LANES, COL_IKR // LANES), _row_spec(ts, LANES, COL_IW // LANES),
                  gspec, gspec, gspec] + [tspec] * 6,
        out_specs=[_row_spec(ts, w, 0), tspec, tspec, _row_spec(ts, w, 0), tspec, tspec],
        out_shape=[wide, narrow, narrow, wide, narrow,
                   jax.ShapeDtypeStruct((s, LANES), F32)],
        compiler_params=_params(("arbitrary",)),
        name="prep_b",
    )(proj, proj, proj, proj, proj, proj, gq, gk, gik, *tab_b, *tab_a)


def _prep_c(proj, kv, gqn, gqr, gkn, gkr, tab, *, ts):
    s = proj.shape[0]
    wn, wr = C_HEADS * C_NOPE, C_HEADS * C_ROPE
    wqk = C_HEADS * 2 * LANES
    tspec = _row_spec(ts, LANES, 0)
    gspec = _const_spec((1, LANES))
    return pl.pallas_call(
        partial(_prep_c_kernel, q_scale=C_QK ** -0.5 * LOG2E),
        grid=(s // ts,),
        in_specs=[_row_spec(ts, wn, COL_C_QN // wn), _row_spec(ts, wr, COL_C_QR // wr),
                  _row_spec(ts, LANES, COL_IKR // LANES), _row_spec(ts, kv.shape[1], 0),
                  gspec, gspec, gspec, gspec, tspec, tspec, tspec],
        out_specs=[_row_spec(ts, wqk, 0), _row_spec(ts, wqk, 0), _row_spec(ts, C_HEADS * C_V, 0)],
        out_shape=[jax.ShapeDtypeStruct((s, wqk), BF16), jax.ShapeDtypeStruct((s, wqk), BF16),
                   jax.ShapeDtypeStruct((s, C_HEADS * C_V), BF16)],
        compiler_params=_params(("arbitrary",)),
        name="prep_c",
    )(proj, proj, proj, kv, gqn, gqr, gkn, gkr, *tab)


def _flash_step(q, kc, vc, carry, mask):
    m, l, acc = carry
    s = _dot_nt(q, kc)
    if mask is not None:
        s = jnp.where(mask, s, -jnp.inf)
    m_new = jnp.maximum(m, jnp.max(s, axis=-1, keepdims=True))
    alpha = jnp.exp2(m - m_new)
    p = jnp.exp2(s - m_new)
    l = alpha * l + jnp.sum(p, axis=-1, keepdims=True)
    acc = alpha * acc + _dot(p.astype(BF16), vc)
    return m_new, l, acc


def _flash_init(rows, dv, m0):
    return (jnp.full((rows, 1), m0, F32), jnp.zeros((rows, 1), F32), jnp.zeros((rows, dv), F32))


def _causal_flash(q, k_ref, v_ref, i, t, reps):
    rows = reps * t

    def body(c, carry):
        return _flash_step(q, k_ref[c], v_ref[c], carry, None)

    carry = lax.fori_loop(0, i, body, _flash_init(rows, v_ref.shape[-1], -jnp.inf))
    assert reps in (1, 2)
    row = lax.broadcasted_iota(I32, (rows, t), 0)
    if reps == 2:
        row = jnp.where(row >= t, row - t, row)
    col = lax.broadcasted_iota(I32, (rows, t), 1)
    return _flash_step(q, k_ref[i], v_ref[i], carry, col <= row)


def _attn_c_kernel(q_ref, k_ref, v_ref, o_ref, *, t):
    _, l, acc = _causal_flash(q_ref[...], k_ref, v_ref, pl.program_id(1), t, 1)
    o_ref[...] = (acc / l).astype(o_ref.dtype)


def _attn_c(cq, ck, cv, *, t):
    s = cq.shape[0]
    nc = s // t
    dq = 2 * LANES
    ck3 = ck.reshape(nc, t, ck.shape[1])
    cv3 = cv.reshape(nc, t, cv.shape[1])
    return pl.pallas_call(
        partial(_attn_c_kernel, t=t),
        grid=(C_HEADS, nc),
        in_specs=[pl.BlockSpec((t, dq), lambda h, i: (i, h)),
                  pl.BlockSpec((nc, t, dq), lambda h, i: (0, 0, h)),
                  pl.BlockSpec((nc, t, C_V), lambda h, i: (0, 0, h))],
        out_specs=pl.BlockSpec((t, C_V), lambda h, i: (i, h)),
        out_shape=jax.ShapeDtypeStruct((s, C_HEADS * C_V), BF16),
        compiler_params=_params(("arbitrary", "arbitrary")),
        name="attn_c",
    )(cq, ck3, cv3)


def _attn_a_kernel(lam_ref, g_ref, q_ref, k_ref, v_ref, o_ref, *, t, lam_init):
    q = q_ref[...]
    low = _lane(q.shape) < A_QK
    zero = jnp.zeros_like(q)
    qq = jnp.concatenate([jnp.where(low, q, zero), jnp.where(low, zero, q)], axis=0)
    _, l, acc = _causal_flash(qq, k_ref, v_ref, pl.program_id(1), t, 2)
    lf = lam_ref[...]
    lam = (jnp.exp(jnp.sum(lf[0:1, :] * lf[1:2, :], axis=-1, keepdims=True))
           - jnp.exp(jnp.sum(lf[2:3, :] * lf[3:4, :], axis=-1, keepdims=True)) + lam_init)
    o = acc[:t] / l[:t] - lam * (acc[t:] / l[t:])
    o_ref[...] = (_norm128(o, g_ref[...]) * (1.0 - lam_init)).astype(o_ref.dtype)


def _attn_a(aq, ak, av, a_lambda, g_sub, *, t, lam_init):
    s = aq.shape[0]
    nc = s // t
    ak3 = ak.reshape(nc, t, ak.shape[1])
    av3 = av.reshape(nc, t, av.shape[1])
    return pl.pallas_call(
        partial(_attn_a_kernel, t=t, lam_init=lam_init),
        grid=(A_HEADS, nc),
        in_specs=[pl.BlockSpec(a_lambda.shape, lambda h, i: (0, 0)),
                  pl.BlockSpec((1, LANES), lambda h, i: (0, 0)),
                  pl.BlockSpec((t, LANES), lambda h, i: (i, h)),
                  pl.BlockSpec((nc, t, LANES), lambda h, i: (0, 0, h)),
                  pl.BlockSpec((nc, t, A_V), lambda h, i: (0, 0, h))],
        out_specs=pl.BlockSpec((t, A_V), lambda h, i: (i, h)),
        out_shape=jax.ShapeDtypeStruct((s, A_HEADS * A_V), BF16),
        compiler_params=_params(("arbitrary", "arbitrary")),
        name="attn_a",
    )(a_lambda, g_sub, aq, ak3, av3)


def _level_of_key(key):
    bits = key ^ (lax.shift_right_arithmetic(key, 31) & 0x7FFFFFFF)
    return lax.bitcast_convert_type(bits, F32)


KEY_LOWEST_FINITE = -2139095040


ROW_BLOCK = 128


def _attn_b_kernel(iq_ref, iw_ref, ik_ref, q_ref, k_ref, v_ref, o_ref,
                   score_ref, iqs_ref, qs_ref, *, tq, tk, top_k):
    i = pl.program_id(0)
    nfull = (i * tq) // tk
    nchunks = nfull + 1
    dv = v_ref.shape[-1]
    neg = -1e30

    iq = iq_ref[...]
    low = _lane((tq, LANES)) < I_DIM
    for j in range(I_HEADS // 2):
        blk = iq[:, j * LANES:(j + 1) * LANES]
        zero = jnp.zeros_like(blk)
        iqs_ref[(2 * j) * tq:(2 * j + 1) * tq, :] = jnp.where(low, blk, zero)
        iqs_ref[(2 * j + 1) * tq:(2 * j + 2) * tq, :] = jnp.where(low, zero, blk)
    for h in range(B_HEADS):
        qs_ref[h * tq:(h + 1) * tq, :] = q_ref[:, h * B_DIM:(h + 1) * B_DIM]
    iw = iw_ref[...]
    ws = [iw[:, h:h + 1] for h in range(I_HEADS)]

    def score_chunk(c):
        s = _dot_nt(iqs_ref[...], ik_ref[c])
        score = ws[0] * jnp.maximum(s[0:tq], 0.0)
        for h in range(1, I_HEADS):
            score += ws[h] * jnp.maximum(s[h * tq:(h + 1) * tq], 0.0)
        return score

    def score_body(c, _):
        score_ref[c] = score_chunk(c)
        return 0

    lax.fori_loop(0, nfull, score_body, 0)
    row_g = i * tq + lax.broadcasted_iota(I32, (tq, tk), 0)
    col_g = nfull * tk + lax.broadcasted_iota(I32, (tq, tk), 1)
    score_ref[nfull] = jnp.where(col_g <= row_g, score_chunk(nfull), -jnp.inf)

    ones_sq = jnp.ones((LANES, LANES), BF16)

    def count(compare, level):
        accs = []
        for r0 in range(0, tq, ROW_BLOCK):
            lvl = level[r0:r0 + ROW_BLOCK]

            def body(c, acc, r0=r0, lvl=lvl):
                sc = score_ref[c, r0:r0 + ROW_BLOCK, :]
                for g in range(tk // LANES):
                    acc += jnp.where(compare(sc[:, g * LANES:(g + 1) * LANES], lvl), 1, 0)
                return acc

            accs.append(lax.fori_loop(0, nchunks, body, jnp.zeros((ROW_BLOCK, LANES), I32)))
        return _dot(jnp.concatenate(accs, axis=0).astype(F32).astype(BF16), ones_sq)

    c0 = count(lambda s, t: s >= t, jnp.zeros((tq, LANES), F32))
    ok0 = c0 >= top_k
    key0 = jnp.where(ok0, 0, INT_MIN).astype(I32)
    cnt0 = jnp.where(ok0, c0, (nchunks * tk).astype(F32))

    def bisect(b, carry):
        key, cnt = carry
        cand = key + lax.shift_left(jnp.int32(1), 30 - b)
        c = count(lambda s, t: s >= t, _level_of_key(cand))
        ok = c >= top_k
        return jnp.where(ok, cand, key), jnp.where(ok, c, cnt)

    key_l, cnt_l = lax.fori_loop(0, 31, bisect, (key0, cnt0))
    thr_l = _level_of_key(jnp.maximum(key_l, KEY_LOWEST_FINITE))
    thr = thr_l[:, 0:1]
    rows = i * tq + lax.broadcasted_iota(I32, (tq, 1), 0)
    has_ties = jnp.max(jnp.where(rows < top_k, 0.0, cnt_l[:, 0:1] - top_k)) > 0.0

    qs = qs_ref[...]
    nrow = B_HEADS * tq

    def flash(c, sel, state):
        m, l, acc = state
        s = _dot_nt(qs, k_ref[c])
        s = jnp.concatenate([jnp.where(sel, s[h * tq:(h + 1) * tq], neg) for h in range(B_HEADS)],
                            axis=0)
        m_new = jnp.maximum(m, jnp.max(s, axis=-1, keepdims=True))
        alpha = jnp.exp2(m - m_new)
        p = jnp.exp2(s - m_new)
        l = alpha * l + jnp.sum(p, axis=-1, keepdims=True)
        acc = alpha * acc + _dot(p.astype(BF16), v_ref[c])
        return m_new, l, acc

    init = _flash_init(nrow, dv, neg)

    def run_exact_count():
        def body(c, state):
            return flash(c, score_ref[c] >= thr, state)
        return lax.fori_loop(0, nchunks, body, init)

    def run_with_ties():
        need = top_k - count(lambda s, t: s > t, thr_l)[:, 0:1]
        tri = (lax.broadcasted_iota(I32, (tk, tk), 0)
               <= lax.broadcasted_iota(I32, (tk, tk), 1)).astype(BF16)

        def body(c, carry):
            seen, state = carry
            sc = score_ref[c]
            tie = sc == thr
            tie_f = jnp.where(tie, 1.0, 0.0)
            rank = seen + _dot(tie_f.astype(BF16), tri)
            sel = (sc > thr) | (tie & (rank <= need))
            seen = seen + jnp.sum(tie_f, axis=-1, keepdims=True)
            return seen, flash(c, sel, state)

        _, state = lax.fori_loop(0, nchunks, body, (jnp.zeros((tq, 1), F32), init))
        return state

    _, l, acc = lax.cond(has_ties, run_with_ties, run_exact_count)
    out = acc / l
    for h in range(B_HEADS):
        o_ref[:, h * dv:(h + 1) * dv] = out[h * tq:(h + 1) * tq].astype(o_ref.dtype)


def _attn_b(iq, iw, ik2, bq, bk, bv, *, tq, tk, top_k):
    s = bq.shape[0]
    assert tk % tq == 0 and tq % ROW_BLOCK == 0 and tk >= top_k and s // LANES <= 256
    nc = s // tk
    ik3 = ik2.reshape(nc, tk, LANES)
    bk3 = bk.reshape(nc, tk, B_DIM)
    bv3 = bv.reshape(nc, tk, B_DIM)
    w = B_HEADS * B_DIM
    whole = lambda shape: pl.BlockSpec(shape, lambda i: (0, 0, 0))
    return pl.pallas_call(
        partial(_attn_b_kernel, tq=tq, tk=tk, top_k=top_k),
        grid=(s // tq,),
        in_specs=[pl.BlockSpec((tq, I_HEADS * I_DIM), lambda i: (i, 0)),
                  pl.BlockSpec((tq, LANES), lambda i: (i, 0)),
                  whole((nc, tk, LANES)),
                  pl.BlockSpec((tq, w), lambda i: (i, 0)),
                  whole((nc, tk, B_DIM)), whole((nc, tk, B_DIM))],
        out_specs=pl.BlockSpec((tq, w), lambda i: (i, 0)),
        out_shape=jax.ShapeDtypeStruct((s, w), BF16),
        scratch_shapes=[pltpu.VMEM((nc, tq, tk), F32),
                        pltpu.VMEM((I_HEADS * tq, LANES), BF16),
                        pltpu.VMEM((B_HEADS * tq, B_DIM), BF16)],
        compiler_params=_params(("arbitrary",)),
        name="attn_b",
    )(iq, iw, ik3, bq, bk3, bv3)


def _rope_tables(seq, group, rot):
    half = rot // 2
    pos = jnp.arange(seq, dtype=F32)
    inv_freq = ROPE_THETA ** (-jnp.arange(half, dtype=F32) * 2.0 / rot)
    ang = pos[:, None] * inv_freq[None, :]
    cos, sin = jnp.cos(ang), jnp.sin(ang)
    pad = group - rot
    ones = jnp.ones((seq, pad), F32)
    zeros_h = jnp.zeros((seq, half), F32)
    zeros_p = jnp.zeros((seq, pad), F32)
    c = jnp.concatenate([cos, cos, ones], axis=1)
    up = jnp.concatenate([-sin, zeros_h, zeros_p], axis=1)
    dn = jnp.concatenate([zeros_h, sin, zeros_p], axis=1)
    reps = LANES // group
    return tuple(jnp.tile(t, (1, reps)) for t in (c, up, dn))


def _pack_w_in(w):
    d = w.shape[0]
    widths = (512, 512, 512, 512, 128, 128, 512, 64, 8, C_HEADS * C_QK, C_RANK, C_ROPE,
              N_BRANCH * d)
    cuts, acc = [], 0
    for wd in widths[:-1]:
        acc += wd
        cuts.append(acc)
    edges = [0] + cuts + [w.shape[1]]
    (a_q, a_k, a_v, b_q, b_k, b_v, i_q, i_k, i_w, c_q, c_kv, c_kr, gates) = [
        w[:, lo:hi].astype(BF16) for lo, hi in zip(edges[:-1], edges[1:])]
    c_q = c_q.reshape(d, C_HEADS, C_QK)
    c_qr = c_q[:, :, :C_ROPE].reshape(d, C_HEADS * C_ROPE)
    c_qn = c_q[:, :, C_ROPE:].reshape(d, C_HEADS * C_NOPE)
    i_w = jnp.pad(i_w, ((0, 0), (0, LANES - I_HEADS)))
    return jnp.concatenate([a_q, a_k, a_v, b_q, i_q, c_kv, c_qn, c_qr, b_k, b_v, i_k, c_kr,
                            i_w, gates], axis=1)


def _row(v):
    return v.reshape(1, -1).astype(F32)


def _tile_sizes(s):
    big = 1024 if s % 1024 == 0 else 512
    return dict(tm=big, t_attn=big, tq_b=256, tk_b=big, ts=512)


def kernel(x, attn_norm, w_in, a_q_norm, a_k_norm, a_lambda, a_sub_norm, b_q_norm, b_k_norm,
           idx_k_norm, c_q_norm, c_kv_norm, w_kv_b, c_k_norm, w_branch, w_out, ffn_norm,
           w_gate_up, w_down):
    bsz, seq, d = x.shape
    assert bsz == 1 and seq % 512 == 0
    depth = w_in.shape[0]
    ts = _tile_sizes(seq)
    top_k = min(TOPK_MAX, seq // 4)
    d_ff = w_down.shape[1]

    tab_a = _rope_tables(seq, A_QK, A_QK // 4)
    tab_b = _rope_tables(seq, B_DIM, B_DIM // 4)
    tab_c = _rope_tables(seq, C_ROPE, C_ROPE)
    zeros64 = jnp.zeros((64,), F32)

    h = x.reshape(seq, d)
    for l in range(depth):
        lam_init = 0.8 - 0.6 * math.exp(-0.3 * l)
        proj = _norm_matmul(h, _row(attn_norm[l]), _pack_w_in(w_in[l]), x_col_block=0, k=d,
                            tm=ts["tm"], tn=1024, out_dtype=F32, name="proj_in")
        kv = _norm_matmul(proj, _row(c_kv_norm[l]), w_kv_b[l].astype(BF16),
                          x_col_block=COL_C_KV // C_RANK, k=C_RANK,
                          tm=ts["tm"], tn=1024, out_dtype=F32, name="kv_up")

        aq, ak, av = _prep_a(proj, _row(jnp.tile(a_q_norm[l], 2)), _row(jnp.tile(a_k_norm[l], 2)),
                             tab_a, ts=ts["ts"])
        bq, bk, bv, iq, ik2, iw = _prep_b(
            proj, _row(b_q_norm[l]), _row(b_k_norm[l]),
            _row(jnp.concatenate([idx_k_norm[l], zeros64])), tab_b, tab_a, ts=ts["ts"])
        cq, ck, cv = _prep_c(
            proj, kv, _row(c_q_norm[l][C_ROPE:]), _row(jnp.tile(c_q_norm[l][:C_ROPE], 2)),
            _row(c_k_norm[l][C_ROPE:]), _row(jnp.concatenate([c_k_norm[l][:C_ROPE], zeros64])),
            tab_c, ts=ts["ts"])

        a_o = _attn_a(aq, ak, av, a_lambda[l].astype(F32), _row(a_sub_norm[l]),
                      t=ts["t_attn"], lam_init=lam_init)
        b_o = _attn_b(iq, iw, ik2, bq, bk, bv, tq=ts["tq_b"], tk=ts["tk_b"], top_k=top_k)
        c_o = _attn_c(cq, ck, cv, t=ts["t_attn"])

        merged = _merge(a_o, b_o, c_o, w_branch[l].astype(BF16), proj, d=d, tm=512, tn=1024)
        h = _matmul_residual(merged, w_out[l].astype(BF16), h, tm=ts["tm"], tn=1024, name="out_proj")

        hff = _ffn_up(h, _row(ffn_norm[l]), w_gate_up[l].astype(BF16), tm=ts["tm"], tn=512)
        h = _matmul_residual(hff, w_down[l].astype(BF16), h, tm=512, tn=512, name="ffn_down")
    return h.reshape(bsz, seq, d)
```

```python
import math
from functools import partial

import jax
import jax.numpy as jnp
from jax import lax
from jax.experimental import pallas as pl
from jax.experimental.pallas import tpu as pltpu

F32 = jnp.float32
BF16 = jnp.bfloat16
I32 = jnp.int32

ROPE_THETA = 500000.0
NORM_EPS = 1e-6
LOG2E = math.log2(math.e)
LANES = 128
VMEM_LIMIT = 56 * 1024 * 1024

A_HEADS, A_QK, A_V = 4, 64, 128
B_HEADS, B_DIM = 4, 128
I_HEADS, I_DIM, TOPK_MAX = 8, 64, 256
C_HEADS, C_NOPE, C_ROPE, C_V, C_RANK = 8, 128, 64, 128, 512
C_QK = C_NOPE + C_ROPE
N_BRANCH = 3
INT_MIN = -2 ** 31

COL_A_Q, COL_A_K, COL_A_V, COL_B_Q, COL_I_Q = 0, 512, 1024, 1536, 2048
COL_C_KV, COL_C_QN, COL_C_QR = 2560, 3072, 4096
COL_B_K, COL_B_V, COL_IKR, COL_IW, COL_G = 4608, 4736, 4864, 4992, 5120


def _params(sem):
    return pltpu.CompilerParams(dimension_semantics=sem, vmem_limit_bytes=VMEM_LIMIT)


def _dot(a, b):
    return jnp.dot(a, b, preferred_element_type=F32)


def _dot_nt(a, b):
    return lax.dot_general(a, b, (((1,), (1,)), ((), ())), preferred_element_type=F32)


def _rms_rows(x, g):
    ms = jnp.mean(x * x, axis=-1, keepdims=True)
    return x * lax.rsqrt(ms + NORM_EPS) * g


def _norm_matmul_kernel(x_ref, g_ref, w_ref, o_ref, xn_ref):
    @pl.when(pl.program_id(1) == 0)
    def _():
        xn_ref[...] = _rms_rows(x_ref[...].astype(F32), g_ref[...]).astype(BF16)

    o_ref[...] = _dot(xn_ref[...], w_ref[...]).astype(o_ref.dtype)


def _norm_matmul(x, g, w, *, x_col_block, k, tm, tn, out_dtype, name):
    s = x.shape[0]
    n = w.shape[1]
    return pl.pallas_call(
        _norm_matmul_kernel,
        grid=(s // tm, n // tn),
        in_specs=[
            pl.BlockSpec((tm, k), lambda i, j: (i, x_col_block)),
            pl.BlockSpec((1, k), lambda i, j: (0, 0)),
            pl.BlockSpec((k, tn), lambda i, j: (0, j)),
        ],
        out_specs=pl.BlockSpec((tm, tn), lambda i, j: (i, j)),
        out_shape=jax.ShapeDtypeStruct((s, n), out_dtype),
        scratch_shapes=[pltpu.VMEM((tm, k), BF16)],
        compiler_params=_params(("arbitrary", "arbitrary")),
        name=name,
    )(x, g, w)


def _ffn_up_kernel(x_ref, g_ref, wg_ref, wu_ref, o_ref, xn_ref):
    @pl.when(pl.program_id(1) == 0)
    def _():
        xn_ref[...] = _rms_rows(x_ref[...], g_ref[...]).astype(BF16)

    xn = xn_ref[...]
    gate = _dot(xn, wg_ref[...].astype(BF16))
    up = _dot(xn, wu_ref[...].astype(BF16))
    o_ref[...] = (gate * (1.0 / (1.0 + jnp.exp(-gate))) * up).astype(o_ref.dtype)


def _ffn_up(x, g, w_gate_up, *, tm, tn):
    s, d = x.shape
    d_ff = w_gate_up.shape[1] // 2
    nj = d_ff // tn
    return pl.pallas_call(
        _ffn_up_kernel,
        grid=(s // tm, nj),
        in_specs=[
            pl.BlockSpec((tm, d), lambda i, j: (i, 0)),
            pl.BlockSpec((1, d), lambda i, j: (0, 0)),
            pl.BlockSpec((d, tn), lambda i, j: (0, j)),
            pl.BlockSpec((d, tn), lambda i, j: (0, j + nj)),
        ],
        out_specs=pl.BlockSpec((tm, tn), lambda i, j: (i, j)),
        out_shape=jax.ShapeDtypeStruct((s, d_ff), BF16),
        scratch_shapes=[pltpu.VMEM((tm, d), BF16)],
        compiler_params=_params(("arbitrary", "arbitrary")),
        name="ffn_up",
    )(x, g, w_gate_up, w_gate_up)


def _matmul_residual_kernel(a_ref, w_ref, r_ref, o_ref):
    o_ref[...] = r_ref[...] + _dot(a_ref[...], w_ref[...].astype(BF16))


def _matmul_residual(a, w, res, *, tm, tn, name):
    s, k = a.shape
    n = w.shape[1]
    return pl.pallas_call(
        _matmul_residual_kernel,
        grid=(s // tm, n // tn),
        in_specs=[
            pl.BlockSpec((tm, k), lambda i, j: (i, 0)),
            pl.BlockSpec((k, tn), lambda i, j: (0, j)),
            pl.BlockSpec((tm, tn), lambda i, j: (i, j)),
        ],
        out_specs=pl.BlockSpec((tm, tn), lambda i, j: (i, j)),
        out_shape=jax.ShapeDtypeStruct((s, n), F32),
        compiler_params=_params(("arbitrary", "arbitrary")),
        name=name,
    )(a, w, res)


def _sigmoid(x):
    return 1.0 / (1.0 + jnp.exp(-x))


def _merge_kernel(a_ref, b_ref, c_ref, wa_ref, wb_ref, wc_ref, ga_ref, gb_ref, gc_ref, o_ref):
    merged = _sigmoid(ga_ref[...]) * _dot(a_ref[...], wa_ref[...])
    merged += _sigmoid(gb_ref[...]) * _dot(b_ref[...], wb_ref[...])
    merged += _sigmoid(gc_ref[...]) * _dot(c_ref[...], wc_ref[...])
    o_ref[...] = merged.astype(o_ref.dtype)


def _merge(a_o, b_o, c_o, w_branch, proj, *, d, tm, tn):
    s = a_o.shape[0]
    a_w, b_w, c_w = a_o.shape[1], b_o.shape[1], c_o.shape[1]
    assert a_w == b_w and c_w == a_w + b_w
    g0 = COL_G // tn
    gstep = d // tn
    return pl.pallas_call(
        _merge_kernel,
        grid=(s // tm, d // tn),
        in_specs=[
            pl.BlockSpec((tm, a_w), lambda i, j: (i, 0)),
            pl.BlockSpec((tm, b_w), lambda i, j: (i, 0)),
            pl.BlockSpec((tm, c_w), lambda i, j: (i, 0)),
            pl.BlockSpec((a_w, tn), lambda i, j: (0, j)),
            pl.BlockSpec((b_w, tn), lambda i, j: (1, j)),
            pl.BlockSpec((c_w, tn), lambda i, j: (1, j)),
            pl.BlockSpec((tm, tn), lambda i, j: (i, g0 + j)),
            pl.BlockSpec((tm, tn), lambda i, j: (i, g0 + gstep + j)),
            pl.BlockSpec((tm, tn), lambda i, j: (i, g0 + 2 * gstep + j)),
        ],
        out_specs=pl.BlockSpec((tm, tn), lambda i, j: (i, j)),
        out_shape=jax.ShapeDtypeStruct((s, d), BF16),
        compiler_params=_params(("arbitrary", "arbitrary")),
        name="merge",
    )(a_o, b_o, c_o, w_branch, w_branch, w_branch, proj, proj, proj)


def _lane(shape):
    return lax.broadcasted_iota(I32, shape, 1)


def _half_sums(sq):
    low = _lane(sq.shape) < 64
    lo = jnp.sum(jnp.where(low, sq, 0.0), axis=-1, keepdims=True)
    hi = jnp.sum(jnp.where(low, 0.0, sq), axis=-1, keepdims=True)
    return lo, hi


def _norm64(x, g):
    lo, hi = _half_sums(x * x)
    ms = jnp.where(_lane(x.shape) < 64, lo, hi) * (1.0 / 64.0)
    return x * lax.rsqrt(ms + NORM_EPS) * g


def _norm128(x, g):
    ms = jnp.mean(x * x, axis=-1, keepdims=True)
    return x * lax.rsqrt(ms + NORM_EPS) * g


def _rope(y, cos, sin_up, sin_dn, half):
    return (y * cos + pltpu.roll(y, LANES - half, 1) * sin_up
            + pltpu.roll(y, half, 1) * sin_dn)


def _chunk(ref, c):
    return ref[:, c * LANES:(c + 1) * LANES]


def _prep_a_kernel(q_ref, k_ref, v_ref, gq_ref, gk_ref, cos_ref, su_ref, sd_ref,
                   oq_ref, ok_ref, ov_ref, *, q_scale):
    cos, su, sd = cos_ref[...], su_ref[...], sd_ref[...]
    half = A_QK // 8
    for c in range(A_HEADS):
        sl = slice(c * LANES, (c + 1) * LANES)
        q = _rope(_norm64(_chunk(q_ref, c), gq_ref[...]), cos, su, sd, half)
        k = _rope(_norm64(_chunk(k_ref, c), gk_ref[...]), cos, su, sd, half)
        oq_ref[:, sl] = (q * q_scale).astype(BF16)
        ok_ref[:, sl] = k.astype(BF16)
    ov_ref[...] = v_ref[...].astype(BF16)


def _prep_b_kernel(q_ref, k_ref, v_ref, iq_ref, ikr_ref, iw_ref,
                   gq_ref, gk_ref, gik_ref,
                   cb_ref, sub_ref, sdb_ref, ca_ref, sua_ref, sda_ref,
                   oq_ref, ok_ref, ov_ref, oiq_ref, oik_ref, oiw_ref, *, q_scale, w_scale):
    cb, sub, sdb = cb_ref[...], sub_ref[...], sdb_ref[...]
    ca, sua, sda = ca_ref[...], sua_ref[...], sda_ref[...]
    b_half = B_DIM // 8
    i_half = I_DIM // 8
    for c in range(B_HEADS):
        sl = slice(c * LANES, (c + 1) * LANES)
        q = _rope(_norm128(_chunk(q_ref, c), gq_ref[...]), cb, sub, sdb, b_half)
        oq_ref[:, sl] = (q * q_scale).astype(BF16)
        oiq_ref[:, sl] = _rope(_chunk(iq_ref, c), ca, sua, sda, i_half).astype(BF16)
    ok_ref[...] = _rope(_norm128(k_ref[...], gk_ref[...]), cb, sub, sdb, b_half).astype(BF16)
    ov_ref[...] = v_ref[...].astype(BF16)
    x = ikr_ref[...]
    low = _lane(x.shape) < 64
    ik = _rope(_norm64(x, gik_ref[...]), ca, sua, sda, i_half)
    ik = jnp.where(low, ik, 0.0)
    oik_ref[...] = (ik + pltpu.roll(ik, 64, 1)).astype(BF16)
    oiw_ref[...] = iw_ref[...] * w_scale


def _prep_c_kernel(qn_ref, qr_ref, ikr_ref, kv_ref,
                   gqn_ref, gqr_ref, gkn_ref, gkr_ref, cos_ref, su_ref, sd_ref,
                   oq_ref, ok_ref, ov_ref, *, q_scale):
    cos, su, sd = cos_ref[...], su_ref[...], sd_ref[...]
    half = C_ROPE // 2
    low = _lane(cos.shape) < 64
    inv_d = 1.0 / C_QK

    for c in range(C_HEADS // 2):
        r = _chunk(qr_ref, c)
        lo, hi = _half_sums(r * r)
        rr = _rope(r * gqr_ref[...], cos, su, sd, half)
        parts = (jnp.where(low, rr, 0.0), pltpu.roll(jnp.where(low, 0.0, rr), 64, 1))
        for j, (ss_r, rope_part) in enumerate(zip((lo, hi), parts)):
            h = 2 * c + j
            n = _chunk(qn_ref, h)
            ss = jnp.sum(n * n, axis=-1, keepdims=True) + ss_r
            sc = lax.rsqrt(ss * inv_d + NORM_EPS) * q_scale
            oq_ref[:, 2 * h * LANES:(2 * h + 1) * LANES] = (n * sc * gqn_ref[...]).astype(BF16)
            oq_ref[:, (2 * h + 1) * LANES:(2 * h + 2) * LANES] = (rope_part * sc).astype(BF16)

    x = ikr_ref[...]
    kr = pltpu.roll(jnp.where(_lane(x.shape) < 64, 0.0, x), 64, 1)
    ss_r = jnp.sum(kr * kr, axis=-1, keepdims=True)
    kr = _rope(kr * gkr_ref[...], cos, su, sd, half)
    for h in range(C_HEADS):
        n = _chunk(kv_ref, 2 * h)
        ss = jnp.sum(n * n, axis=-1, keepdims=True) + ss_r
        sc = lax.rsqrt(ss * inv_d + NORM_EPS)
        ok_ref[:, 2 * h * LANES:(2 * h + 1) * LANES] = (n * sc * gkn_ref[...]).astype(BF16)
        ok_ref[:, (2 * h + 1) * LANES:(2 * h + 2) * LANES] = (kr * sc).astype(BF16)
        ov_ref[:, h * C_V:(h + 1) * C_V] = _chunk(kv_ref, 2 * h + 1).astype(BF16)


def _row_spec(ts, width, col_block):
    return pl.BlockSpec((ts, width), lambda i: (i, col_block))


def _const_spec(shape):
    return pl.BlockSpec(shape, lambda i: (0,) * len(shape))


def _prep_a(proj, gq, gk, tab, *, ts):
    s = proj.shape[0]
    w = A_HEADS * 2 * A_QK
    tspec = _row_spec(ts, LANES, 0)
    out = jax.ShapeDtypeStruct((s, w), BF16)
    return pl.pallas_call(
        partial(_prep_a_kernel, q_scale=A_QK ** -0.5 * LOG2E),
        grid=(s // ts,),
        in_specs=[_row_spec(ts, w, COL_A_Q // w), _row_spec(ts, w, COL_A_K // w),
                  _row_spec(ts, w, COL_A_V // w),
                  _const_spec((1, LANES)), _const_spec((1, LANES)), tspec, tspec, tspec],
        out_specs=[_row_spec(ts, w, 0)] * 3,
        out_shape=[out, out, out],
        compiler_params=_params(("arbitrary",)),
        name="prep_a",
    )(proj, proj, proj, gq, gk, *tab)


def _prep_b(proj, gq, gk, gik, tab_b, tab_a, *, ts):
    s = proj.shape[0]
    w = B_HEADS * B_DIM
    tspec = _row_spec(ts, LANES, 0)
    gspec = _const_spec((1, LANES))
    wide = jax.ShapeDtypeStruct((s, w), BF16)
    narrow = jax.ShapeDtypeStruct((s, LANES), BF16)
    return pl.pallas_call(
        partial(_prep_b_kernel, q_scale=B_DIM ** -0.5 * LOG2E,
                w_scale=I_HEADS ** -0.5 * I_DIM ** -0.5),
        grid=(s // ts,),
        in_specs=[_row_spec(ts, w, COL_B_Q // w), _row_spec(ts, LANES, COL_B_K // LANES),
                  _row_spec(ts, LANES, COL_B_V // LANES), _row_spec(ts, w, COL_I_Q // w),
                  _row_spec(ts, LANES, COL_IKR // LANES), _row_spec(ts, LANES, COL_IW // LANES),
                  gspec, gspec, gspec] + [tspec] * 6,
        out_specs=[_row_spec(ts, w, 0), tspec, tspec, _row_spec(ts, w, 0), tspec, tspec],
        out_shape=[wide, narrow, narrow, wide, narrow,
                   jax.ShapeDtypeStruct((s, LANES), F32)],
        compiler_params=_params(("arbitrary",)),
        name="prep_b",
    )(proj, proj, proj, proj, proj, proj, gq, gk, gik, *tab_b, *tab_a)


def _prep_c(proj, kv, gqn, gqr, gkn, gkr, tab, *, ts):
    s = proj.shape[0]
    wn, wr = C_HEADS * C_NOPE, C_HEADS * C_ROPE
    wqk = C_HEADS * 2 * LANES
    tspec = _row_spec(ts, LANES, 0)
    gspec = _const_spec((1, LANES))
    return pl.pallas_call(
        partial(_prep_c_kernel, q_scale=C_QK ** -0.5 * LOG2E),
        grid=(s // ts,),
        in_specs=[_row_spec(ts, wn, COL_C_QN // wn), _row_spec(ts, wr, COL_C_QR // wr),
                  _row_spec(ts, LANES, COL_IKR // LANES), _row_spec(ts, kv.shape[1], 0),
                  gspec, gspec, gspec, gspec, tspec, tspec, tspec],
        out_specs=[_row_spec(ts, wqk, 0), _row_spec(ts, wqk, 0), _row_spec(ts, C_HEADS * C_V, 0)],
        out_shape=[jax.ShapeDtypeStruct((s, wqk), BF16), jax.ShapeDtypeStruct((s, wqk), BF16),
                   jax.ShapeDtypeStruct((s, C_HEADS * C_V), BF16)],
        compiler_params=_params(("arbitrary",)),
        name="prep_c",
    )(proj, proj, proj, kv, gqn, gqr, gkn, gkr, *tab)


def _flash_step(q, kc, vc, carry, mask):
    m, l, acc = carry
    s = _dot_nt(q, kc)
    if mask is not None:
        s = jnp.where(mask, s, -jnp.inf)
    m_new = jnp.maximum(m, jnp.max(s, axis=-1, keepdims=True))
    alpha = jnp.exp2(m - m_new)
    p = jnp.exp2(s - m_new)
    l = alpha * l + jnp.sum(p, axis=-1, keepdims=True)
    acc = alpha * acc + _dot(p.astype(BF16), vc)
    return m_new, l, acc


def _flash_init(rows, dv, m0):
    return (jnp.full((rows, 1), m0, F32), jnp.zeros((rows, 1), F32), jnp.zeros((rows, dv), F32))


def _causal_flash(q, k_ref, v_ref, i, t, reps):
    rows = reps * t

    def body(c, carry):
        return _flash_step(q, k_ref[c], v_ref[c], carry, None)

    carry = lax.fori_loop(0, i, body, _flash_init(rows, v_ref.shape[-1], -jnp.inf))
    assert reps in (1, 2)
    row = lax.broadcasted_iota(I32, (rows, t), 0)
    if reps == 2:
        row = jnp.where(row >= t, row - t, row)
    col = lax.broadcasted_iota(I32, (rows, t), 1)
    return _flash_step(q, k_ref[i], v_ref[i], carry, col <= row)


def _attn_c_kernel(q_ref, k_ref, v_ref, o_ref, *, t):
    _, l, acc = _causal_flash(q_ref[...], k_ref, v_ref, pl.program_id(1), t, 1)
    o_ref[...] = (acc / l).astype(o_ref.dtype)


def _attn_c(cq, ck, cv, *, t):
    s = cq.shape[0]
    nc = s // t
    dq = 2 * LANES
    ck3 = ck.reshape(nc, t, ck.shape[1])
    cv3 = cv.reshape(nc, t, cv.shape[1])
    return pl.pallas_call(
        partial(_attn_c_kernel, t=t),
        grid=(C_HEADS, nc),
        in_specs=[pl.BlockSpec((t, dq), lambda h, i: (i, h)),
                  pl.BlockSpec((nc, t, dq), lambda h, i: (0, 0, h)),
                  pl.BlockSpec((nc, t, C_V), lambda h, i: (0, 0, h))],
        out_specs=pl.BlockSpec((t, C_V), lambda h, i: (i, h)),
        out_shape=jax.ShapeDtypeStruct((s, C_HEADS * C_V), BF16),
        compiler_params=_params(("arbitrary", "arbitrary")),
        name="attn_c",
    )(cq, ck3, cv3)


def _attn_a_kernel(lam_ref, g_ref, q_ref, k_ref, v_ref, o_ref, *, t, lam_init):
    q = q_ref[...]
    low = _lane(q.shape) < A_QK
    zero = jnp.zeros_like(q)
    qq = jnp.concatenate([jnp.where(low, q, zero), jnp.where(low, zero, q)], axis=0)
    _, l, acc = _causal_flash(qq, k_ref, v_ref, pl.program_id(1), t, 2)
    lf = lam_ref[...]
    lam = (jnp.exp(jnp.sum(lf[0:1, :] * lf[1:2, :], axis=-1, keepdims=True))
           - jnp.exp(jnp.sum(lf[2:3, :] * lf[3:4, :], axis=-1, keepdims=True)) + lam_init)
    o = acc[:t] / l[:t] - lam * (acc[t:] / l[t:])
    o_ref[...] = (_norm128(o, g_ref[...]) * (1.0 - lam_init)).astype(o_ref.dtype)


def _attn_a(aq, ak, av, a_lambda, g_sub, *, t, lam_init):
    s = aq.shape[0]
    nc = s // t
    ak3 = ak.reshape(nc, t, ak.shape[1])
    av3 = av.reshape(nc, t, av.shape[1])
    return pl.pallas_call(
        partial(_attn_a_kernel, t=t, lam_init=lam_init),
        grid=(A_HEADS, nc),
        in_specs=[pl.BlockSpec(a_lambda.shape, lambda h, i: (0, 0)),
                  pl.BlockSpec((1, LANES), lambda h, i: (0, 0)),
                  pl.BlockSpec((t, LANES), lambda h, i: (i, h)),
                  pl.BlockSpec((nc, t, LANES), lambda h, i: (0, 0, h)),
                  pl.BlockSpec((nc, t, A_V), lambda h, i: (0, 0, h))],
        out_specs=pl.BlockSpec((t, A_V), lambda h, i: (i, h)),
        out_shape=jax.ShapeDtypeStruct((s, A_HEADS * A_V), BF16),
        compiler_params=_params(("arbitrary", "arbitrary")),
        name="attn_a",
    )(a_lambda, g_sub, aq, ak3, av3)


def _level_of_key(key):
    bits = key ^ (lax.shift_right_arithmetic(key, 31) & 0x7FFFFFFF)
    return lax.bitcast_convert_type(bits, F32)


KEY_LOWEST_FINITE = -2139095040


ROW_BLOCK = 128


def _attn_b_kernel(iq_ref, iw_ref, ik_ref, q_ref, k_ref, v_ref, o_ref,
                   score_ref, iqs_ref, qs_ref, *, tq, tk, top_k):
    i = pl.program_id(0)
    nfull = (i * tq) // tk
    nchunks = nfull + 1
    dv = v_ref.shape[-1]
    neg = -1e30

    iq = iq_ref[...]
    low = _lane((tq, LANES)) < I_DIM
    for j in range(I_HEADS // 2):
        blk = iq[:, j * LANES:(j + 1) * LANES]
        zero = jnp.zeros_like(blk)
        iqs_ref[(2 * j) * tq:(2 * j + 1) * tq, :] = jnp.where(low, blk, zero)
        iqs_ref[(2 * j + 1) * tq:(2 * j + 2) * tq, :] = jnp.where(low, zero, blk)
    for h in range(B_HEADS):
        qs_ref[h * tq:(h + 1) * tq, :] = q_ref[:, h * B_DIM:(h + 1) * B_DIM]
    iw = iw_ref[...]
    ws = [iw[:, h:h + 1] for h in range(I_HEADS)]

    def score_chunk(c):
        s = _dot_nt(iqs_ref[...], ik_ref[c])
        score = ws[0] * jnp.maximum(s[0:tq], 0.0)
        for h in range(1, I_HEADS):
            score += ws[h] * jnp.maximum(s[h * tq:(h + 1) * tq], 0.0)
        return score

    def score_body(c, _):
        score_ref[c] = score_chunk(c)
        return 0

    lax.fori_loop(0, nfull, score_body, 0)
    row_g = i * tq + lax.broadcasted_iota(I32, (tq, tk), 0)
    col_g = nfull * tk + lax.broadcasted_iota(I32, (tq, tk), 1)
    score_ref[nfull] = jnp.where(col_g <= row_g, score_chunk(nfull), -jnp.inf)

    ones_sq = jnp.ones((LANES, LANES), BF16)

    def count(compare, level):
        accs = []
        for r0 in range(0, tq, ROW_BLOCK):
            lvl = level[r0:r0 + ROW_BLOCK]

            def body(c, acc, r0=r0, lvl=lvl):
                sc = score_ref[c, r0:r0 + ROW_BLOCK, :]
                for g in range(tk // LANES):
                    acc += jnp.where(compare(sc[:, g * LANES:(g + 1) * LANES], lvl), 1, 0)
                return acc

            accs.append(lax.fori_loop(0, nchunks, body, jnp.zeros((ROW_BLOCK, LANES), I32)))
        return _dot(jnp.concatenate(accs, axis=0).astype(F32).astype(BF16), ones_sq)

    c0 = count(lambda s, t: s >= t, jnp.zeros((tq, LANES), F32))
    ok0 = c0 >= top_k
    key0 = jnp.where(ok0, 0, INT_MIN).astype(I32)
    cnt0 = jnp.where(ok0, c0, (nchunks * tk).astype(F32))

    def bisect(b, carry):
        key, cnt = carry
        cand = key + lax.shift_left(jnp.int32(1), 30 - b)
        c = count(lambda s, t: s >= t, _level_of_key(cand))
        ok = c >= top_k
        return jnp.where(ok, cand, key), jnp.where(ok, c, cnt)

    key_l, cnt_l = lax.fori_loop(0, 31, bisect, (key0, cnt0))
    thr_l = _level_of_key(jnp.maximum(key_l, KEY_LOWEST_FINITE))
    thr = thr_l[:, 0:1]
    rows = i * tq + lax.broadcasted_iota(I32, (tq, 1), 0)
    has_ties = jnp.max(jnp.where(rows < top_k, 0.0, cnt_l[:, 0:1] - top_k)) > 0.0

    qs = qs_ref[...]
    nrow = B_HEADS * tq

    def flash(c, sel, state):
        m, l, acc = state
        s = _dot_nt(qs, k_ref[c])
        s = jnp.concatenate([jnp.where(sel, s[h * tq:(h + 1) * tq], neg) for h in range(B_HEADS)],
                            axis=0)
        m_new = jnp.maximum(m, jnp.max(s, axis=-1, keepdims=True))
        alpha = jnp.exp2(m - m_new)
        p = jnp.exp2(s - m_new)
        l = alpha * l + jnp.sum(p, axis=-1, keepdims=True)
        acc = alpha * acc + _dot(p.astype(BF16), v_ref[c])
        return m_new, l, acc

    init = _flash_init(nrow, dv, neg)

    def run_exact_count():
        def body(c, state):
            return flash(c, score_ref[c] >= thr, state)
        return lax.fori_loop(0, nchunks, body, init)

    def run_with_ties():
        need = top_k - count(lambda s, t: s > t, thr_l)[:, 0:1]
        tri = (lax.broadcasted_iota(I32, (tk, tk), 0)
               <= lax.broadcasted_iota(I32, (tk, tk), 1)).astype(BF16)

        def body(c, carry):
            seen, state = carry
            sc = score_ref[c]
            tie = sc == thr
            tie_f = jnp.where(tie, 1.0, 0.0)
            rank = seen + _dot(tie_f.astype(BF16), tri)
            sel = (sc > thr) | (tie & (rank <= need))
            seen = seen + jnp.sum(tie_f, axis=-1, keepdims=True)
            return seen, flash(c, sel, state)

        _, state = lax.fori_loop(0, nchunks, body, (jnp.zeros((tq, 1), F32), init))
        return state

    _, l, acc = lax.cond(has_ties, run_with_ties, run_exact_count)
    out = acc / l
    for h in range(B_HEADS):
        o_ref[:, h * dv:(h + 1) * dv] = out[h * tq:(h + 1) * tq].astype(o_ref.dtype)


def _attn_b(iq, iw, ik2, bq, bk, bv, *, tq, tk, top_k):
    s = bq.shape[0]
    assert tk % tq == 0 and tq % ROW_BLOCK == 0 and tk >= top_k and s // LANES <= 256
    nc = s // tk
    ik3 = ik2.reshape(nc, tk, LANES)
    bk3 = bk.reshape(nc, tk, B_DIM)
    bv3 = bv.reshape(nc, tk, B_DIM)
    w = B_HEADS * B_DIM
    whole = lambda shape: pl.BlockSpec(shape, lambda i: (0, 0, 0))
    return pl.pallas_call(
        partial(_attn_b_kernel, tq=tq, tk=tk, top_k=top_k),
        grid=(s // tq,),
        in_specs=[pl.BlockSpec((tq, I_HEADS * I_DIM), lambda i: (i, 0)),
                  pl.BlockSpec((tq, LANES), lambda i: (i, 0)),
                  whole((nc, tk, LANES)),
                  pl.BlockSpec((tq, w), lambda i: (i, 0)),
                  whole((nc, tk, B_DIM)), whole((nc, tk, B_DIM))],
        out_specs=pl.BlockSpec((tq, w), lambda i: (i, 0)),
        out_shape=jax.ShapeDtypeStruct((s, w), BF16),
        scratch_shapes=[pltpu.VMEM((nc, tq, tk), F32),
                        pltpu.VMEM((I_HEADS * tq, LANES), BF16),
                        pltpu.VMEM((B_HEADS * tq, B_DIM), BF16)],
        compiler_params=_params(("arbitrary",)),
        name="attn_b",
    )(iq, iw, ik3, bq, bk3, bv3)


def _rope_tables(seq, group, rot):
    half = rot // 2
    pos = jnp.arange(seq, dtype=F32)
    inv_freq = ROPE_THETA ** (-jnp.arange(half, dtype=F32) * 2.0 / rot)
    ang = pos[:, None] * inv_freq[None, :]
    cos, sin = jnp.cos(ang), jnp.sin(ang)
    pad = group - rot
    ones = jnp.ones((seq, pad), F32)
    zeros_h = jnp.zeros((seq, half), F32)
    zeros_p = jnp.zeros((seq, pad), F32)
    c = jnp.concatenate([cos, cos, ones], axis=1)
    up = jnp.concatenate([-sin, zeros_h, zeros_p], axis=1)
    dn = jnp.concatenate([zeros_h, sin, zeros_p], axis=1)
    reps = LANES // group
    return tuple(jnp.tile(t, (1, reps)) for t in (c, up, dn))


def _pack_w_in(w):
    d = w.shape[0]
    widths = (512, 512, 512, 512, 128, 128, 512, 64, 8, C_HEADS * C_QK, C_RANK, C_ROPE,
              N_BRANCH * d)
    cuts, acc = [], 0
    for wd in widths[:-1]:
        acc += wd
        cuts.append(acc)
    edges = [0] + cuts + [w.shape[1]]
    (a_q, a_k, a_v, b_q, b_k, b_v, i_q, i_k, i_w, c_q, c_kv, c_kr, gates) = [
        w[:, lo:hi].astype(BF16) for lo, hi in zip(edges[:-1], edges[1:])]
    c_q = c_q.reshape(d, C_HEADS, C_QK)
    c_qr = c_q[:, :, :C_ROPE].reshape(d, C_HEADS * C_ROPE)
    c_qn = c_q[:, :, C_ROPE:].reshape(d, C_HEADS * C_NOPE)
    i_w = jnp.pad(i_w, ((0, 0), (0, LANES - I_HEADS)))
    return jnp.concatenate([a_q, a_k, a_v, b_q, i_q, c_kv, c_qn, c_qr, b_k, b_v, i_k, c_kr,
                            i_w, gates], axis=1)


def _row(v):
    return v.reshape(1, -1).astype(F32)


def _tile_sizes(s):
    big = 1024 if s % 1024 == 0 else 512
    return dict(tm=big, t_attn=big, tq_b=256, tk_b=big, ts=512)


def kernel(x, attn_norm, w_in, a_q_norm, a_k_norm, a_lambda, a_sub_norm, b_q_norm, b_k_norm,
           idx_k_norm, c_q_norm, c_kv_norm, w_kv_b, c_k_norm, w_branch, w_out, ffn_norm,
           w_gate_up, w_down):
    bsz, seq, d = x.shape
    assert bsz == 1 and seq % 512 == 0
    depth = w_in.shape[0]
    ts = _tile_sizes(seq)
    top_k = min(TOPK_MAX, seq // 4)
    d_ff = w_down.shape[1]

    tab_a = _rope_tables(seq, A_QK, A_QK // 4)
    tab_b = _rope_tables(seq, B_DIM, B_DIM // 4)
    tab_c = _rope_tables(seq, C_ROPE, C_ROPE)
    zeros64 = jnp.zeros((64,), F32)

    h = x.reshape(seq, d)
    for l in range(depth):
        lam_init = 0.8 - 0.6 * math.exp(-0.3 * l)
        proj = _norm_matmul(h, _row(attn_norm[l]), _pack_w_in(w_in[l]), x_col_block=0, k=d,
                            tm=ts["tm"], tn=1024, out_dtype=F32, name="proj_in")
        kv = _norm_matmul(proj, _row(c_kv_norm[l]), w_kv_b[l].astype(BF16),
                          x_col_block=COL_C_KV // C_RANK, k=C_RANK,
                          tm=ts["tm"], tn=1024, out_dtype=F32, name="kv_up")

        aq, ak, av = _prep_a(proj, _row(jnp.tile(a_q_norm[l], 2)), _row(jnp.tile(a_k_norm[l], 2)),
                             tab_a, ts=ts["ts"])
        bq, bk, bv, iq, ik2, iw = _prep_b(
            proj, _row(b_q_norm[l]), _row(b_k_norm[l]),
            _row(jnp.concatenate([idx_k_norm[l], zeros64])), tab_b, tab_a, ts=ts["ts"])
        cq, ck, cv = _prep_c(
            proj, kv, _row(c_q_norm[l][C_ROPE:]), _row(jnp.tile(c_q_norm[l][:C_ROPE], 2)),
            _row(c_k_norm[l][C_ROPE:]), _row(jnp.concatenate([c_k_norm[l][:C_ROPE], zeros64])),
            tab_c, ts=ts["ts"])

        a_o = _attn_a(aq, ak, av, a_lambda[l].astype(F32), _row(a_sub_norm[l]),
                      t=ts["t_attn"], lam_init=lam_init)
        b_o = _attn_b(iq, iw, ik2, bq, bk, bv, tq=ts["tq_b"], tk=ts["tk_b"], top_k=top_k)
        c_o = _attn_c(cq, ck, cv, t=ts["t_attn"])

        merged = _merge(a_o, b_o, c_o, w_branch[l].astype(BF16), proj, d=d, tm=512, tn=1024)
        h = _matmul_residual(merged, w_out[l], h, tm=ts["tm"], tn=1024, name="out_proj")

        hff = _ffn_up(h, _row(ffn_norm[l]), w_gate_up[l], tm=ts["tm"], tn=512)
        h = _matmul_residual(hff, w_down[l].astype(BF16), h, tm=512, tn=512, name="ffn_down")
    return h.reshape(bsz, seq, d)
```

```python
import math
from functools import partial

import jax
import jax.numpy as jnp
from jax import lax
from jax.experimental import pallas as pl
from jax.experimental.pallas import tpu as pltpu

F32 = jnp.float32
BF16 = jnp.bfloat16
I32 = jnp.int32

ROPE_THETA = 500000.0
NORM_EPS = 1e-6
LOG2E = math.log2(math.e)
LANES = 128
VMEM_LIMIT = 56 * 1024 * 1024

A_HEADS, A_QK, A_V = 4, 64, 128
B_HEADS, B_DIM = 4, 128
I_HEADS, I_DIM, TOPK_MAX = 8, 64, 256
C_HEADS, C_NOPE, C_ROPE, C_V, C_RANK = 8, 128, 64, 128, 512
C_QK = C_NOPE + C_ROPE
N_BRANCH = 3
INT_MIN = -2 ** 31

COL_A_Q, COL_A_K, COL_A_V, COL_B_Q, COL_I_Q = 0, 512, 1024, 1536, 2048
COL_C_KV, COL_C_QN, COL_C_QR = 2560, 3072, 4096
COL_B_K, COL_B_V, COL_IKR, COL_IW, COL_G = 4608, 4736, 4864, 4992, 5120


def _params(sem):
    return pltpu.CompilerParams(dimension_semantics=sem, vmem_limit_bytes=VMEM_LIMIT)


def _dot(a, b):
    return jnp.dot(a, b, preferred_element_type=F32)


def _dot_nt(a, b):
    return lax.dot_general(a, b, (((1,), (1,)), ((), ())), preferred_element_type=F32)


def _rms_rows(x, g):
    ms = jnp.mean(x * x, axis=-1, keepdims=True)
    return x * lax.rsqrt(ms + NORM_EPS) * g


def _norm_matmul_kernel(x_ref, g_ref, w_ref, o_ref, xn_ref):
    @pl.when(pl.program_id(1) == 0)
    def _():
        xn_ref[...] = _rms_rows(x_ref[...].astype(F32), g_ref[...]).astype(BF16)

    o_ref[...] = _dot(xn_ref[...], w_ref[...]).astype(o_ref.dtype)


def _norm_matmul(x, g, w, *, x_col_block, k, tm, tn, out_dtype, name):
    s = x.shape[0]
    n = w.shape[1]
    return pl.pallas_call(
        _norm_matmul_kernel,
        grid=(s // tm, n // tn),
        in_specs=[
            pl.BlockSpec((tm, k), lambda i, j: (i, x_col_block)),
            pl.BlockSpec((1, k), lambda i, j: (0, 0)),
            pl.BlockSpec((k, tn), lambda i, j: (0, j)),
        ],
        out_specs=pl.BlockSpec((tm, tn), lambda i, j: (i, j)),
        out_shape=jax.ShapeDtypeStruct((s, n), out_dtype),
        scratch_shapes=[pltpu.VMEM((tm, k), BF16)],
        compiler_params=_params(("arbitrary", "arbitrary")),
        name=name,
    )(x, g, w)


def _ffn_up_kernel(x_ref, g_ref, wg_ref, wu_ref, o_ref, xn_ref):
    @pl.when(pl.program_id(1) == 0)
    def _():
        xn_ref[...] = _rms_rows(x_ref[...], g_ref[...]).astype(BF16)

    xn = xn_ref[...]
    gate = _dot(xn, wg_ref[...].astype(BF16))
    up = _dot(xn, wu_ref[...].astype(BF16))
    o_ref[...] = (gate * (1.0 / (1.0 + jnp.exp(-gate))) * up).astype(o_ref.dtype)


def _ffn_up(x, g, w_gate_up, layer, *, tm, tn):
    s, d = x.shape
    d_ff = w_gate_up.shape[2] // 2
    nj = d_ff // tn
    return pl.pallas_call(
        _ffn_up_kernel,
        grid=(s // tm, nj),
        in_specs=[
            pl.BlockSpec((tm, d), lambda i, j: (i, 0)),
            pl.BlockSpec((1, d), lambda i, j: (0, 0)),
            pl.BlockSpec((None, d, tn), lambda i, j: (layer, 0, j)),
            pl.BlockSpec((None, d, tn), lambda i, j: (layer, 0, j + nj)),
        ],
        out_specs=pl.BlockSpec((tm, tn), lambda i, j: (i, j)),
        out_shape=jax.ShapeDtypeStruct((s, d_ff), BF16),
        scratch_shapes=[pltpu.VMEM((tm, d), BF16)],
        compiler_params=_params(("arbitrary", "arbitrary")),
        name="ffn_up",
    )(x, g, w_gate_up, w_gate_up)


def _matmul_residual_kernel(a_ref, w_ref, r_ref, o_ref):
    o_ref[...] = r_ref[...] + _dot(a_ref[...], w_ref[...].astype(BF16))


def _matmul_residual(a, w, res, *, tm, tn, name, layer=None):
    s, k = a.shape
    n = w.shape[-1]
    if layer is None:
        w_spec = pl.BlockSpec((k, tn), lambda i, j: (0, j))
    else:
        w_spec = pl.BlockSpec((None, k, tn), lambda i, j: (layer, 0, j))
    return pl.pallas_call(
        _matmul_residual_kernel,
        grid=(s // tm, n // tn),
        in_specs=[
            pl.BlockSpec((tm, k), lambda i, j: (i, 0)),
            w_spec,
            pl.BlockSpec((tm, tn), lambda i, j: (i, j)),
        ],
        out_specs=pl.BlockSpec((tm, tn), lambda i, j: (i, j)),
        out_shape=jax.ShapeDtypeStruct((s, n), F32),
        compiler_params=_params(("arbitrary", "arbitrary")),
        name=name,
    )(a, w, res)


def _sigmoid(x):
    return 1.0 / (1.0 + jnp.exp(-x))


def _merge_kernel(a_ref, b_ref, c_ref, wa_ref, wb_ref, wc_ref, ga_ref, gb_ref, gc_ref, o_ref):
    merged = _sigmoid(ga_ref[...]) * _dot(a_ref[...], wa_ref[...])
    merged += _sigmoid(gb_ref[...]) * _dot(b_ref[...], wb_ref[...])
    merged += _sigmoid(gc_ref[...]) * _dot(c_ref[...], wc_ref[...])
    o_ref[...] = merged.astype(o_ref.dtype)


def _merge(a_o, b_o, c_o, w_branch, proj, *, d, tm, tn):
    s = a_o.shape[0]
    a_w, b_w, c_w = a_o.shape[1], b_o.shape[1], c_o.shape[1]
    assert a_w == b_w and c_w == a_w + b_w
    g0 = COL_G // tn
    gstep = d // tn
    return pl.pallas_call(
        _merge_kernel,
        grid=(s // tm, d // tn),
        in_specs=[
            pl.BlockSpec((tm, a_w), lambda i, j: (i, 0)),
            pl.BlockSpec((tm, b_w), lambda i, j: (i, 0)),
            pl.BlockSpec((tm, c_w), lambda i, j: (i, 0)),
            pl.BlockSpec((a_w, tn), lambda i, j: (0, j)),
            pl.BlockSpec((b_w, tn), lambda i, j: (1, j)),
            pl.BlockSpec((c_w, tn), lambda i, j: (1, j)),
            pl.BlockSpec((tm, tn), lambda i, j: (i, g0 + j)),
            pl.BlockSpec((tm, tn), lambda i, j: (i, g0 + gstep + j)),
            pl.BlockSpec((tm, tn), lambda i, j: (i, g0 + 2 * gstep + j)),
        ],
        out_specs=pl.BlockSpec((tm, tn), lambda i, j: (i, j)),
        out_shape=jax.ShapeDtypeStruct((s, d), BF16),
        compiler_params=_params(("arbitrary", "arbitrary")),
        name="merge",
    )(a_o, b_o, c_o, w_branch, w_branch, w_branch, proj, proj, proj)


def _lane(shape):
    return lax.broadcasted_iota(I32, shape, 1)


def _half_sums(sq):
    low = _lane(sq.shape) < 64
    lo = jnp.sum(jnp.where(low, sq, 0.0), axis=-1, keepdims=True)
    hi = jnp.sum(jnp.where(low, 0.0, sq), axis=-1, keepdims=True)
    return lo, hi


def _norm64(x, g):
    lo, hi = _half_sums(x * x)
    ms = jnp.where(_lane(x.shape) < 64, lo, hi) * (1.0 / 64.0)
    return x * lax.rsqrt(ms + NORM_EPS) * g


def _norm128(x, g):
    ms = jnp.mean(x * x, axis=-1, keepdims=True)
    return x * lax.rsqrt(ms + NORM_EPS) * g


def _rope(y, cos, sin_up, sin_dn, half):
    return (y * cos + pltpu.roll(y, LANES - half, 1) * sin_up
            + pltpu.roll(y, half, 1) * sin_dn)


def _chunk(ref, c):
    return ref[:, c * LANES:(c + 1) * LANES]


def _prep_a_kernel(q_ref, k_ref, v_ref, gq_ref, gk_ref, cos_ref, su_ref, sd_ref,
                   oq_ref, ok_ref, ov_ref, *, q_scale):
    cos, su, sd = cos_ref[...], su_ref[...], sd_ref[...]
    half = A_QK // 8
    for c in range(A_HEADS):
        sl = slice(c * LANES, (c + 1) * LANES)
        q = _rope(_norm64(_chunk(q_ref, c), gq_ref[...]), cos, su, sd, half)
        k = _rope(_norm64(_chunk(k_ref, c), gk_ref[...]), cos, su, sd, half)
        oq_ref[:, sl] = (q * q_scale).astype(BF16)
        ok_ref[:, sl] = k.astype(BF16)
    ov_ref[...] = v_ref[...].astype(BF16)


def _prep_b_kernel(q_ref, k_ref, v_ref, iq_ref, ikr_ref, iw_ref,
                   gq_ref, gk_ref, gik_ref,
                   cb_ref, sub_ref, sdb_ref, ca_ref, sua_ref, sda_ref,
                   oq_ref, ok_ref, ov_ref, oiq_ref, oik_ref, oiw_ref, *, q_scale, w_scale):
    cb, sub, sdb = cb_ref[...], sub_ref[...], sdb_ref[...]
    ca, sua, sda = ca_ref[...], sua_ref[...], sda_ref[...]
    b_half = B_DIM // 8
    i_half = I_DIM // 8
    for c in range(B_HEADS):
        sl = slice(c * LANES, (c + 1) * LANES)
        q = _rope(_norm128(_chunk(q_ref, c), gq_ref[...]), cb, sub, sdb, b_half)
        oq_ref[:, sl] = (q * q_scale).astype(BF16)
        oiq_ref[:, sl] = _rope(_chunk(iq_ref, c), ca, sua, sda, i_half).astype(BF16)
    ok_ref[...] = _rope(_norm128(k_ref[...], gk_ref[...]), cb, sub, sdb, b_half).astype(BF16)
    ov_ref[...] = v_ref[...].astype(BF16)
    x = ikr_ref[...]
    low = _lane(x.shape) < 64
    ik = _rope(_norm64(x, gik_ref[...]), ca, sua, sda, i_half)
    ik = jnp.where(low, ik, 0.0)
    oik_ref[...] = (ik + pltpu.roll(ik, 64, 1)).astype(BF16)
    oiw_ref[...] = iw_ref[...] * w_scale


def _prep_c_kernel(qn_ref, qr_ref, ikr_ref, kv_ref,
                   gqn_ref, gqr_ref, gkn_ref, gkr_ref, cos_ref, su_ref, sd_ref,
                   oq_ref, ok_ref, ov_ref, *, q_scale):
    cos, su, sd = cos_ref[...], su_ref[...], sd_ref[...]
    half = C_ROPE // 2
    low = _lane(cos.shape) < 64
    inv_d = 1.0 / C_QK

    for c in range(C_HEADS // 2):
        r = _chunk(qr_ref, c)
        lo, hi = _half_sums(r * r)
        rr = _rope(r * gqr_ref[...], cos, su, sd, half)
        parts = (jnp.where(low, rr, 0.0), pltpu.roll(jnp.where(low, 0.0, rr), 64, 1))
        for j, (ss_r, rope_part) in enumerate(zip((lo, hi), parts)):
            h = 2 * c + j
            n = _chunk(qn_ref, h)
            ss = jnp.sum(n * n, axis=-1, keepdims=True) + ss_r
            sc = lax.rsqrt(ss * inv_d + NORM_EPS) * q_scale
            oq_ref[:, 2 * h * LANES:(2 * h + 1) * LANES] = (n * sc * gqn_ref[...]).astype(BF16)
            oq_ref[:, (2 * h + 1) * LANES:(2 * h + 2) * LANES] = (rope_part * sc).astype(BF16)

    x = ikr_ref[...]
    kr = pltpu.roll(jnp.where(_lane(x.shape) < 64, 0.0, x), 64, 1)
    ss_r = jnp.sum(kr * kr, axis=-1, keepdims=True)
    kr = _rope(kr * gkr_ref[...], cos, su, sd, half)
    for h in range(C_HEADS):
        n = _chunk(kv_ref, 2 * h)
        ss = jnp.sum(n * n, axis=-1, keepdims=True) + ss_r
        sc = lax.rsqrt(ss * inv_d + NORM_EPS)
        ok_ref[:, 2 * h * LANES:(2 * h + 1) * LANES] = (n * sc * gkn_ref[...]).astype(BF16)
        ok_ref[:, (2 * h + 1) * LANES:(2 * h + 2) * LANES] = (kr * sc).astype(BF16)
        ov_ref[:, h * C_V:(h + 1) * C_V] = _chunk(kv_ref, 2 * h + 1).astype(BF16)


def _row_spec(ts, width, col_block):
    return pl.BlockSpec((ts, width), lambda i: (i, col_block))


def _const_spec(shape):
    return pl.BlockSpec(shape, lambda i: (0,) * len(shape))


def _prep_a(proj, gq, gk, tab, *, ts):
    s = proj.shape[0]
    w = A_HEADS * 2 * A_QK
    tspec = _row_spec(ts, LANES, 0)
    out = jax.ShapeDtypeStruct((s, w), BF16)
    return pl.pallas_call(
        partial(_prep_a_kernel, q_scale=A_QK ** -0.5 * LOG2E),
        grid=(s // ts,),
        in_specs=[_row_spec(ts, w, COL_A_Q // w), _row_spec(ts, w, COL_A_K // w),
                  _row_spec(ts, w, COL_A_V // w),
                  _const_spec((1, LANES)), _const_spec((1, LANES)), tspec, tspec, tspec],
        out_specs=[_row_spec(ts, w, 0)] * 3,
        out_shape=[out, out, out],
        compiler_params=_params(("arbitrary",)),
        name="prep_a",
    )(proj, proj, proj, gq, gk, *tab)


def _prep_b(proj, gq, gk, gik, tab_b, tab_a, *, ts):
    s = proj.shape[0]
    w = B_HEADS * B_DIM
    tspec = _row_spec(ts, LANES, 0)
    gspec = _const_spec((1, LANES))
    wide = jax.ShapeDtypeStruct((s, w), BF16)
    narrow = jax.ShapeDtypeStruct((s, LANES), BF16)
    return pl.pallas_call(
        partial(_prep_b_kernel, q_scale=B_DIM ** -0.5 * LOG2E,
                w_scale=I_HEADS ** -0.5 * I_DIM ** -0.5),
        grid=(s // ts,),
        in_specs=[_row_spec(ts, w, COL_B_Q // w), _row_spec(ts, LANES, COL_B_K // LANES),
                  _row_spec(ts, LANES, COL_B_V // LANES), _row_spec(ts, w, COL_I_Q // w),
                  _row_spec(ts, LANES, COL_IKR // LANES), _row_spec(ts, LANES, COL_IW // LANES),
                  gspec, gspec, gspec] + [tspec] * 6,
        out_specs=[_row_spec(ts, w, 0), tspec, tspec, _row_spec(ts, w, 0), tspec, tspec],
        out_shape=[wide, narrow, narrow, wide, narrow,
                   jax.ShapeDtypeStruct((s, LANES), F32)],
        compiler_params=_params(("arbitrary",)),
        name="prep_b",
    )(proj, proj, proj, proj, proj, proj, gq, gk, gik, *tab_b, *tab_a)


def _prep_c(proj, kv, gqn, gqr, gkn, gkr, tab, *, ts):
    s = proj.shape[0]
    wn, wr = C_HEADS * C_NOPE, C_HEADS * C_ROPE
    wqk = C_HEADS * 2 * LANES
    tspec = _row_spec(ts, LANES, 0)
    gspec = _const_spec((1, LANES))
    return pl.pallas_call(
        partial(_prep_c_kernel, q_scale=C_QK ** -0.5 * LOG2E),
        grid=(s // ts,),
        in_specs=[_row_spec(ts, wn, COL_C_QN // wn), _row_spec(ts, wr, COL_C_QR // wr),
                  _row_spec(ts, LANES, COL_IKR // LANES), _row_spec(ts, kv.shape[1], 0),
                  gspec, gspec, gspec, gspec, tspec, tspec, tspec],
        out_specs=[_row_spec(ts, wqk, 0), _row_spec(ts, wqk, 0), _row_spec(ts, C_HEADS * C_V, 0)],
        out_shape=[jax.ShapeDtypeStruct((s, wqk), BF16), jax.ShapeDtypeStruct((s, wqk), BF16),
                   jax.ShapeDtypeStruct((s, C_HEADS * C_V), BF16)],
        compiler_params=_params(("arbitrary",)),
        name="prep_c",
    )(proj, proj, proj, kv, gqn, gqr, gkn, gkr, *tab)


def _flash_step(q, kc, vc, carry, mask):
    m, l, acc = carry
    s = _dot_nt(q, kc)
    if mask is not None:
        s = jnp.where(mask, s, -jnp.inf)
    m_new = jnp.maximum(m, jnp.max(s, axis=-1, keepdims=True))
    alpha = jnp.exp2(m - m_new)
    p = jnp.exp2(s - m_new)
    l = alpha * l + jnp.sum(p, axis=-1, keepdims=True)
    acc = alpha * acc + _dot(p.astype(BF16), vc)
    return m_new, l, acc


def _flash_init(rows, dv, m0):
    return (jnp.full((rows, 1), m0, F32), jnp.zeros((rows, 1), F32), jnp.zeros((rows, dv), F32))


def _causal_flash(q, k_ref, v_ref, i, t, reps):
    rows = reps * t

    def body(c, carry):
        return _flash_step(q, k_ref[c], v_ref[c], carry, None)

    carry = lax.fori_loop(0, i, body, _flash_init(rows, v_ref.shape[-1], -jnp.inf))
    assert reps in (1, 2)
    row = lax.broadcasted_iota(I32, (rows, t), 0)
    if reps == 2:
        row = jnp.where(row >= t, row - t, row)
    col = lax.broadcasted_iota(I32, (rows, t), 1)
    return _flash_step(q, k_ref[i], v_ref[i], carry, col <= row)


def _attn_c_kernel(q_ref, k_ref, v_ref, o_ref, *, t):
    _, l, acc = _causal_flash(q_ref[...], k_ref, v_ref, pl.program_id(1), t, 1)
    o_ref[...] = (acc / l).astype(o_ref.dtype)


def _attn_c(cq, ck, cv, *, t):
    s = cq.shape[0]
    nc = s // t
    dq = 2 * LANES
    ck3 = ck.reshape(nc, t, ck.shape[1])
    cv3 = cv.reshape(nc, t, cv.shape[1])
    return pl.pallas_call(
        partial(_attn_c_kernel, t=t),
        grid=(C_HEADS, nc),
        in_specs=[pl.BlockSpec((t, dq), lambda h, i: (i, h)),
                  pl.BlockSpec((nc, t, dq), lambda h, i: (0, 0, h)),
                  pl.BlockSpec((nc, t, C_V), lambda h, i: (0, 0, h))],
        out_specs=pl.BlockSpec((t, C_V), lambda h, i: (i, h)),
        out_shape=jax.ShapeDtypeStruct((s, C_HEADS * C_V), BF16),
        compiler_params=_params(("arbitrary", "arbitrary")),
        name="attn_c",
    )(cq, ck3, cv3)


def _attn_a_kernel(lam_ref, g_ref, q_ref, k_ref, v_ref, o_ref, *, t, lam_init):
    q = q_ref[...]
    low = _lane(q.shape) < A_QK
    zero = jnp.zeros_like(q)
    qq = jnp.concatenate([jnp.where(low, q, zero), jnp.where(low, zero, q)], axis=0)
    _, l, acc = _causal_flash(qq, k_ref, v_ref, pl.program_id(1), t, 2)
    lf = lam_ref[...]
    lam = (jnp.exp(jnp.sum(lf[0:1, :] * lf[1:2, :], axis=-1, keepdims=True))
           - jnp.exp(jnp.sum(lf[2:3, :] * lf[3:4, :], axis=-1, keepdims=True)) + lam_init)
    o = acc[:t] / l[:t] - lam * (acc[t:] / l[t:])
    o_ref[...] = (_norm128(o, g_ref[...]) * (1.0 - lam_init)).astype(o_ref.dtype)


def _attn_a(aq, ak, av, a_lambda, g_sub, *, t, lam_init):
    s = aq.shape[0]
    nc = s // t
    ak3 = ak.reshape(nc, t, ak.shape[1])
    av3 = av.reshape(nc, t, av.shape[1])
    return pl.pallas_call(
        partial(_attn_a_kernel, t=t, lam_init=lam_init),
        grid=(A_HEADS, nc),
        in_specs=[pl.BlockSpec(a_lambda.shape, lambda h, i: (0, 0)),
                  pl.BlockSpec((1, LANES), lambda h, i: (0, 0)),
                  pl.BlockSpec((t, LANES), lambda h, i: (i, h)),
                  pl.BlockSpec((nc, t, LANES), lambda h, i: (0, 0, h)),
                  pl.BlockSpec((nc, t, A_V), lambda h, i: (0, 0, h))],
        out_specs=pl.BlockSpec((t, A_V), lambda h, i: (i, h)),
        out_shape=jax.ShapeDtypeStruct((s, A_HEADS * A_V), BF16),
        compiler_params=_params(("arbitrary", "arbitrary")),
        name="attn_a",
    )(a_lambda, g_sub, aq, ak3, av3)


def _level_of_key(key):
    bits = key ^ (lax.shift_right_arithmetic(key, 31) & 0x7FFFFFFF)
    return lax.bitcast_convert_type(bits, F32)


KEY_LOWEST_FINITE = -2139095040


ROW_BLOCK = 128


def _attn_b_kernel(iq_ref, iw_ref, ik_ref, q_ref, k_ref, v_ref, o_ref,
                   score_ref, iqs_ref, qs_ref, *, tq, tk, top_k):
    i = pl.program_id(0)
    nfull = (i * tq) // tk
    nchunks = nfull + 1
    dv = v_ref.shape[-1]
    neg = -1e30

    iq = iq_ref[...]
    low = _lane((tq, LANES)) < I_DIM
    for j in range(I_HEADS // 2):
        blk = iq[:, j * LANES:(j + 1) * LANES]
        zero = jnp.zeros_like(blk)
        iqs_ref[(2 * j) * tq:(2 * j + 1) * tq, :] = jnp.where(low, blk, zero)
        iqs_ref[(2 * j + 1) * tq:(2 * j + 2) * tq, :] = jnp.where(low, zero, blk)
    for h in range(B_HEADS):
        qs_ref[h * tq:(h + 1) * tq, :] = q_ref[:, h * B_DIM:(h + 1) * B_DIM]
    iw = iw_ref[...]
    ws = [iw[:, h:h + 1] for h in range(I_HEADS)]

    def score_chunk(c):
        s = _dot_nt(iqs_ref[...], ik_ref[c])
        score = ws[0] * jnp.maximum(s[0:tq], 0.0)
        for h in range(1, I_HEADS):
            score += ws[h] * jnp.maximum(s[h * tq:(h + 1) * tq], 0.0)
        return score

    def score_body(c, _):
        score_ref[c] = score_chunk(c)
        return 0

    lax.fori_loop(0, nfull, score_body, 0)
    row_g = i * tq + lax.broadcasted_iota(I32, (tq, tk), 0)
    col_g = nfull * tk + lax.broadcasted_iota(I32, (tq, tk), 1)
    score_ref[nfull] = jnp.where(col_g <= row_g, score_chunk(nfull), -jnp.inf)

    ones_sq = jnp.ones((LANES, LANES), BF16)

    def count(compare, level):
        accs = []
        for r0 in range(0, tq, ROW_BLOCK):
            lvl = level[r0:r0 + ROW_BLOCK]

            def body(c, acc, r0=r0, lvl=lvl):
                sc = score_ref[c, r0:r0 + ROW_BLOCK, :]
                for g in range(tk // LANES):
                    acc += jnp.where(compare(sc[:, g * LANES:(g + 1) * LANES], lvl), 1, 0)
                return acc

            accs.append(lax.fori_loop(0, nchunks, body, jnp.zeros((ROW_BLOCK, LANES), I32)))
        return _dot(jnp.concatenate(accs, axis=0).astype(F32).astype(BF16), ones_sq)

    c0 = count(lambda s, t: s >= t, jnp.zeros((tq, LANES), F32))
    ok0 = c0 >= top_k
    key0 = jnp.where(ok0, 0, INT_MIN).astype(I32)
    cnt0 = jnp.where(ok0, c0, (nchunks * tk).astype(F32))

    def bisect(b, carry):
        key, cnt = carry
        cand = key + lax.shift_left(jnp.int32(1), 30 - b)
        c = count(lambda s, t: s >= t, _level_of_key(cand))
        ok = c >= top_k
        return jnp.where(ok, cand, key), jnp.where(ok, c, cnt)

    key_l, cnt_l = lax.fori_loop(0, 31, bisect, (key0, cnt0))
    thr_l = _level_of_key(jnp.maximum(key_l, KEY_LOWEST_FINITE))
    thr = thr_l[:, 0:1]
    rows = i * tq + lax.broadcasted_iota(I32, (tq, 1), 0)
    has_ties = jnp.max(jnp.where(rows < top_k, 0.0, cnt_l[:, 0:1] - top_k)) > 0.0

    qs = qs_ref[...]
    nrow = B_HEADS * tq

    def flash(c, sel, state):
        m, l, acc = state
        s = _dot_nt(qs, k_ref[c])
        s = jnp.concatenate([jnp.where(sel, s[h * tq:(h + 1) * tq], neg) for h in range(B_HEADS)],
                            axis=0)
        m_new = jnp.maximum(m, jnp.max(s, axis=-1, keepdims=True))
        alpha = jnp.exp2(m - m_new)
        p = jnp.exp2(s - m_new)
        l = alpha * l + jnp.sum(p, axis=-1, keepdims=True)
        acc = alpha * acc + _dot(p.astype(BF16), v_ref[c])
        return m_new, l, acc

    init = _flash_init(nrow, dv, neg)

    def run_exact_count():
        def body(c, state):
            return flash(c, score_ref[c] >= thr, state)
        return lax.fori_loop(0, nchunks, body, init)

    def run_with_ties():
        need = top_k - count(lambda s, t: s > t, thr_l)[:, 0:1]
        tri = (lax.broadcasted_iota(I32, (tk, tk), 0)
               <= lax.broadcasted_iota(I32, (tk, tk), 1)).astype(BF16)

        def body(c, carry):
            seen, state = carry
            sc = score_ref[c]
            tie = sc == thr
            tie_f = jnp.where(tie, 1.0, 0.0)
            rank = seen + _dot(tie_f.astype(BF16), tri)
            sel = (sc > thr) | (tie & (rank <= need))
            seen = seen + jnp.sum(tie_f, axis=-1, keepdims=True)
            return seen, flash(c, sel, state)

        _, state = lax.fori_loop(0, nchunks, body, (jnp.zeros((tq, 1), F32), init))
        return state

    _, l, acc = lax.cond(has_ties, run_with_ties, run_exact_count)
    out = acc / l
    for h in range(B_HEADS):
        o_ref[:, h * dv:(h + 1) * dv] = out[h * tq:(h + 1) * tq].astype(o_ref.dtype)


def _attn_b(iq, iw, ik2, bq, bk, bv, *, tq, tk, top_k):
    s = bq.shape[0]
    assert tk % tq == 0 and tq % ROW_BLOCK == 0 and tk >= top_k and s // LANES <= 256
    nc = s // tk
    ik3 = ik2.reshape(nc, tk, LANES)
    bk3 = bk.reshape(nc, tk, B_DIM)
    bv3 = bv.reshape(nc, tk, B_DIM)
    w = B_HEADS * B_DIM
    whole = lambda shape: pl.BlockSpec(shape, lambda i: (0, 0, 0))
    return pl.pallas_call(
        partial(_attn_b_kernel, tq=tq, tk=tk, top_k=top_k),
        grid=(s // tq,),
        in_specs=[pl.BlockSpec((tq, I_HEADS * I_DIM), lambda i: (i, 0)),
                  pl.BlockSpec((tq, LANES), lambda i: (i, 0)),
                  whole((nc, tk, LANES)),
                  pl.BlockSpec((tq, w), lambda i: (i, 0)),
                  whole((nc, tk, B_DIM)), whole((nc, tk, B_DIM))],
        out_specs=pl.BlockSpec((tq, w), lambda i: (i, 0)),
        out_shape=jax.ShapeDtypeStruct((s, w), BF16),
        scratch_shapes=[pltpu.VMEM((nc, tq, tk), F32),
                        pltpu.VMEM((I_HEADS * tq, LANES), BF16),
                        pltpu.VMEM((B_HEADS * tq, B_DIM), BF16)],
        compiler_params=_params(("arbitrary",)),
        name="attn_b",
    )(iq, iw, ik3, bq, bk3, bv3)


def _rope_tables(seq, group, rot):
    half = rot // 2
    pos = jnp.arange(seq, dtype=F32)
    inv_freq = ROPE_THETA ** (-jnp.arange(half, dtype=F32) * 2.0 / rot)
    ang = pos[:, None] * inv_freq[None, :]
    cos, sin = jnp.cos(ang), jnp.sin(ang)
    pad = group - rot
    ones = jnp.ones((seq, pad), F32)
    zeros_h = jnp.zeros((seq, half), F32)
    zeros_p = jnp.zeros((seq, pad), F32)
    c = jnp.concatenate([cos, cos, ones], axis=1)
    up = jnp.concatenate([-sin, zeros_h, zeros_p], axis=1)
    dn = jnp.concatenate([zeros_h, sin, zeros_p], axis=1)
    reps = LANES // group
    return tuple(jnp.tile(t, (1, reps)) for t in (c, up, dn))


def _pack_w_in(w):
    d = w.shape[0]
    widths = (512, 512, 512, 512, 128, 128, 512, 64, 8, C_HEADS * C_QK, C_RANK, C_ROPE,
              N_BRANCH * d)
    cuts, acc = [], 0
    for wd in widths[:-1]:
        acc += wd
        cuts.append(acc)
    edges = [0] + cuts + [w.shape[1]]
    (a_q, a_k, a_v, b_q, b_k, b_v, i_q, i_k, i_w, c_q, c_kv, c_kr, gates) = [
        w[:, lo:hi].astype(BF16) for lo, hi in zip(edges[:-1], edges[1:])]
    c_q = c_q.reshape(d, C_HEADS, C_QK)
    c_qr = c_q[:, :, :C_ROPE].reshape(d, C_HEADS * C_ROPE)
    c_qn = c_q[:, :, C_ROPE:].reshape(d, C_HEADS * C_NOPE)
    i_w = jnp.pad(i_w, ((0, 0), (0, LANES - I_HEADS)))
    return jnp.concatenate([a_q, a_k, a_v, b_q, i_q, c_kv, c_qn, c_qr, b_k, b_v, i_k, c_kr,
                            i_w, gates], axis=1)


def _row(v):
    return v.reshape(1, -1).astype(F32)


def _tile_sizes(s):
    big = 1024 if s % 1024 == 0 else 512
    return dict(tm=big, t_attn=big, tq_b=256, tk_b=big, ts=512)


def kernel(x, attn_norm, w_in, a_q_norm, a_k_norm, a_lambda, a_sub_norm, b_q_norm, b_k_norm,
           idx_k_norm, c_q_norm, c_kv_norm, w_kv_b, c_k_norm, w_branch, w_out, ffn_norm,
           w_gate_up, w_down):
    bsz, seq, d = x.shape
    assert bsz == 1 and seq % 512 == 0
    depth = w_in.shape[0]
    ts = _tile_sizes(seq)
    top_k = min(TOPK_MAX, seq // 4)
    d_ff = w_down.shape[1]

    tab_a = _rope_tables(seq, A_QK, A_QK // 4)
    tab_b = _rope_tables(seq, B_DIM, B_DIM // 4)
    tab_c = _rope_tables(seq, C_ROPE, C_ROPE)
    zeros64 = jnp.zeros((64,), F32)

    h = x.reshape(seq, d)
    for l in range(depth):
        lam_init = 0.8 - 0.6 * math.exp(-0.3 * l)
        proj = _norm_matmul(h, _row(attn_norm[l]), _pack_w_in(w_in[l]), x_col_block=0, k=d,
                            tm=ts["tm"], tn=1024, out_dtype=F32, name="proj_in")
        kv = _norm_matmul(proj, _row(c_kv_norm[l]), w_kv_b[l].astype(BF16),
                          x_col_block=COL_C_KV // C_RANK, k=C_RANK,
                          tm=ts["tm"], tn=1024, out_dtype=F32, name="kv_up")

        aq, ak, av = _prep_a(proj, _row(jnp.tile(a_q_norm[l], 2)), _row(jnp.tile(a_k_norm[l], 2)),
                             tab_a, ts=ts["ts"])
        bq, bk, bv, iq, ik2, iw = _prep_b(
            proj, _row(b_q_norm[l]), _row(b_k_norm[l]),
            _row(jnp.concatenate([idx_k_norm[l], zeros64])), tab_b, tab_a, ts=ts["ts"])
        cq, ck, cv = _prep_c(
            proj, kv, _row(c_q_norm[l][C_ROPE:]), _row(jnp.tile(c_q_norm[l][:C_ROPE], 2)),
            _row(c_k_norm[l][C_ROPE:]), _row(jnp.concatenate([c_k_norm[l][:C_ROPE], zeros64])),
            tab_c, ts=ts["ts"])

        a_o = _attn_a(aq, ak, av, a_lambda[l].astype(F32), _row(a_sub_norm[l]),
                      t=ts["t_attn"], lam_init=lam_init)
        b_o = _attn_b(iq, iw, ik2, bq, bk, bv, tq=ts["tq_b"], tk=ts["tk_b"], top_k=top_k)
        c_o = _attn_c(cq, ck, cv, t=ts["t_attn"])

        merged = _merge(a_o, b_o, c_o, w_branch[l].astype(BF16), proj, d=d, tm=512, tn=1024)
        h = _matmul_residual(merged, w_out, h, tm=ts["tm"], tn=1024, name="out_proj", layer=l)

        hff = _ffn_up(h, _row(ffn_norm[l]), w_gate_up, l, tm=ts["tm"], tn=512)
        h = _matmul_residual(hff, w_down[l].astype(BF16), h, tm=512, tn=512, name="ffn_down")
    return h.reshape(bsz, seq, d)
```
